```python
import math
import jax, jax.numpy as jnp
from jax import lax
import numpy as np

D_MODEL = 2048
BATCH = 2
SEQ = 4096
DEPTH = 2

N_META = 16
N_A = DEPTH // 2
N_B = DEPTH - N_A
HG_KDIM = 128
HG_HEADS = D_MODEL // HG_KDIM
HG_FDIM = HG_HEADS * HG_KDIM
HG_VDIM = D_MODEL // HG_HEADS
CHUNK = 64
FOX_HEADS = 16
FOX_HDIM = D_MODEL // FOX_HEADS
Q_BLOCK = 128
EPS = 1e-6
MASK_VALUE = -1e30

kernel_name = "yoco_hgrn2_fox_meta_hybrid"


def rms_norm(x, g):
    xf = x.astype(jnp.float32)
    y = xf * lax.rsqrt(jnp.mean(xf * xf, axis=-1, keepdims=True) + EPS)
    return (y * g.astype(jnp.float32)).astype(x.dtype)


def hgrn2_mix(h, g_norm, w_in, g_out, w_out, lb):
    bsz, L, _ = h.shape
    u = rms_norm(h, g_norm) @ w_in
    q = u[..., :HG_FDIM]
    f = u[..., HG_FDIM:2 * HG_FDIM]
    i = u[..., 2 * HG_FDIM:2 * HG_FDIM + D_MODEL]
    z = u[..., 2 * HG_FDIM + D_MODEL:]
    q = jax.nn.silu(q).astype(jnp.float32)
    fg = lb + (1.0 - lb) * jax.nn.sigmoid(f.astype(jnp.float32))
    logf = jnp.log(fg)
    k = 1.0 - fg
    v = i.astype(jnp.float32)
    n_pad = CHUNK - N_META
    padf = lambda a: jnp.pad(a, ((0, 0), (n_pad, 0), (0, 0)))
    q, logf, k, v = padf(q), padf(logf), padf(k), padf(v)
    Lp = L + n_pad
    nc = Lp // CHUNK
    to_chunks = lambda a, d: a.reshape(bsz, nc, CHUNK, HG_HEADS, d).transpose(0, 3, 1, 2, 4)
    q = to_chunks(q, HG_KDIM)
    logf = to_chunks(logf, HG_KDIM)
    k = to_chunks(k, HG_KDIM)
    v = to_chunks(v, HG_VDIM)
    b = jnp.cumsum(logf, axis=3)
    b_last = b[..., CHUNK - 1:CHUNK, :]
    b_mid = b[..., CHUNK // 2 - 1:CHUNK // 2, :]
    q_intra = q * jnp.exp(b - b_mid)
    k_intra = k * jnp.exp(b_mid - b)
    causal = jnp.tril(jnp.ones((CHUNK, CHUNK), dtype=bool))
    A = jnp.einsum('bhntd,bhnsd->bhnts', q_intra, k_intra)
    A = jnp.where(causal, A, 0.0)
    o_intra = jnp.einsum('bhnts,bhnse->bhnte', A, v)
    dS = jnp.einsum('bhnsd,bhnse->bhnde', k * jnp.exp(b_last - b), v)
    decay = jnp.exp(b_last[..., 0, :])

    def step(S, inp):
        dec, ds = inp
        return dec[..., None] * S + ds, S

    S0 = jnp.zeros((bsz, HG_HEADS, HG_KDIM, HG_VDIM), jnp.float32)
    _, S_prev = lax.scan(step, S0, (jnp.moveaxis(decay, 2, 0), jnp.moveaxis(dS, 2, 0)))
    S_prev = jnp.moveaxis(S_prev, 0, 2)
    o_inter = jnp.einsum('bhntd,bhnde->bhnte', q * jnp.exp(b), S_prev)
    o = (o_intra + o_inter).transpose(0, 2, 3, 1, 4).reshape(bsz, Lp, HG_HEADS, HG_VDIM)
    o = o[:, n_pad:].astype(h.dtype)
    o = rms_norm(o, g_out.reshape(HG_HEADS, HG_VDIM)).reshape(bsz, L, D_MODEL)
    return (o * jax.nn.silu(z)) @ w_out


def fox_shared_kv(h, g_kv, w_kv, b_f, g_k):
    bsz, L, _ = h.shape
    u = rms_norm(h, g_kv) @ w_kv
    k = u[..., :D_MODEL].reshape(bsz, L, FOX_HEADS, FOX_HDIM)
    v = u[..., D_MODEL:2 * D_MODEL].reshape(bsz, L, FOX_HEADS, FOX_HDIM)
    fl = u[..., 2 * D_MODEL:]
    k = rms_norm(k, g_k)
    logf = jax.nn.log_sigmoid(fl.astype(jnp.float32) + b_f.astype(jnp.float32))
    n_pad = Q_BLOCK - N_META
    k = jnp.pad(k, ((0, 0), (n_pad, 0), (0, 0), (0, 0)))
    v = jnp.pad(v, ((0, 0), (n_pad, 0), (0, 0), (0, 0)))
    logf = jnp.pad(logf, ((0, 0), (n_pad, 0), (0, 0)))
    F = jnp.cumsum(logf, axis=1).transpose(0, 2, 1)
    valid = jnp.arange(L + n_pad) >= n_pad
    return k, v, F, valid


def fox_mix(h, g_norm, w_in, g_q, w_out, k, v, F, valid):
    bsz, L, _ = h.shape
    u = rms_norm(h, g_norm) @ w_in
    q = rms_norm(u[..., :D_MODEL].reshape(bsz, L, FOX_HEADS, FOX_HDIM), g_q)
    z = u[..., D_MODEL:]
    n_pad = Q_BLOCK - N_META
    q = jnp.pad(q, ((0, 0), (n_pad, 0), (0, 0), (0, 0)))
    Lp = L + n_pad
    scale = FOX_HDIM ** -0.5
    outs = []
    for blk in range(Lp // Q_BLOCK):
        s0 = blk * Q_BLOCK
        s1 = s0 + Q_BLOCK
        logits = jnp.einsum('bqhd,bkhd->bhqk', q[:, s0:s1], k[:, :s1]).astype(jnp.float32) * scale
        logits = logits + F[:, :, s0:s1, None] - F[:, :, None, :s1]
        qi = jnp.arange(s0, s1)[:, None]
        ki = jnp.arange(s1)[None, :]
        mask = (ki <= qi) & valid[None, :s1]
        logits = jnp.where(mask, logits, MASK_VALUE)
        p = jax.nn.softmax(logits, axis=-1).astype(v.dtype)
        outs.append(jnp.einsum('bhqk,bkhd->bqhd', p, v[:, :s1]))
    o = jnp.concatenate(outs, axis=1)[:, n_pad:].reshape(bsz, L, D_MODEL)
    return (o * jax.nn.silu(z)) @ w_out


def setup_inputs(seed: int = 0) -> dict:
    key = jax.random.key(seed)
    ks = jax.random.split(key, 16)
    f32 = jnp.float32
    nrm = lambda k, shape, s: jax.random.normal(k, shape, f32) * s
    gain = lambda k, shape: 1.0 + 0.02 * jax.random.normal(k, shape, f32)
    sd = D_MODEL ** -0.5
    return {
        "x": nrm(ks[0], (BATCH, SEQ, D_MODEL), 1.0),
        "meta": nrm(ks[1], (N_META, D_MODEL), 1.0),
        "gamma_lb": nrm(ks[2], (N_A + 1, HG_FDIM), 0.1),
        "a_norm": gain(ks[3], (N_A, D_MODEL)),
        "a_w_in": nrm(ks[4], (N_A, D_MODEL, 2 * HG_FDIM + 2 * D_MODEL), sd),
        "a_out_norm": gain(ks[5], (N_A, D_MODEL)),
        "a_w_out": nrm(ks[6], (N_A, D_MODEL, D_MODEL), sd),
        "kv_norm": gain(ks[7], (D_MODEL,)),
        "kv_w": nrm(ks[8], (D_MODEL, 2 * D_MODEL + FOX_HEADS), sd),
        "fox_b_f": 3.0 + 0.1 * jax.random.normal(ks[9], (FOX_HEADS,), f32),
        "fox_k_norm": gain(ks[10], (FOX_HEADS, FOX_HDIM)),
        "b_norm": gain(ks[11], (N_B, D_MODEL)),
        "b_w_in": nrm(ks[12], (N_B, D_MODEL, 2 * D_MODEL), sd),
        "b_q_norm": gain(ks[13], (N_B, FOX_HEADS, FOX_HDIM)),
        "b_w_out": nrm(ks[14], (N_B, D_MODEL, D_MODEL), sd),
    }


def reference(x, meta, gamma_lb, a_norm, a_w_in, a_out_norm, a_w_out, kv_norm, kv_w,
              fox_b_f, fox_k_norm, b_norm, b_w_in, b_q_norm, b_w_out):
    bsz = x.shape[0]
    h = jnp.concatenate(
        [jnp.broadcast_to(meta[None].astype(x.dtype), (bsz, N_META, D_MODEL)), x], axis=1)
    lbs = jnp.cumsum(jax.nn.softmax(gamma_lb.astype(jnp.float32), axis=0), axis=0)
    shared = None
    for layer in range(DEPTH):
        if layer < N_A:
            h = h + hgrn2_mix(h, a_norm[layer], a_w_in[layer], a_out_norm[layer],
                              a_w_out[layer], lbs[layer])
        else:
            if layer == N_A:
                shared = fox_shared_kv(h, kv_norm, kv_w, fox_b_f, fox_k_norm)
            j = layer - N_A
            k_s, v_s, F_s, valid_s = shared
            h = h + fox_mix(h, b_norm[j], b_w_in[j], b_q_norm[j], b_w_out[j],
                            k_s, v_s, F_s, valid_s)
    return h[:, N_META:]
```

```python
import functools

import jax
import jax.numpy as jnp
from jax import lax
from jax.experimental import pallas as pl
from jax.experimental.pallas import tpu as pltpu

F32 = jnp.float32
BF16 = jnp.bfloat16

HEAD_DIM = 128
N_META = 16
Q_BLOCK = 128
PAD_ROWS = Q_BLOCK - N_META
CHUNK = 64
EPS = 1e-6
MASK_VALUE = -1e30
LANES = 128
VMEM_LIMIT_BYTES = 48 * 1024 * 1024


def _pick_tile(n, candidates):
    for c in candidates:
        if n % c == 0:
            return c
    raise ValueError(f"no tile in {candidates} divides {n}")


def _params(semantics):
    return pltpu.CompilerParams(dimension_semantics=semantics, vmem_limit_bytes=VMEM_LIMIT_BYTES)


def _sigmoid(x):
    return 1.0 / (1.0 + jnp.exp(-x))


def _rms(x):
    return x * lax.rsqrt(jnp.mean(x * x, axis=-1, keepdims=True) + EPS)


def _norm_proj_kernel(*refs, mode, scale, with_forget):
    if with_forget:
        x_ref, g_ref, w_ref, hg_ref, wf_ref, bf_ref, o_ref, lf_ref, xn_ref = refs
    elif mode == "headnorm":
        x_ref, g_ref, w_ref, hg_ref, o_ref, xn_ref = refs
    else:
        x_ref, g_ref, w_ref, o_ref, xn_ref = refs

    @pl.when(pl.program_id(2) == 0)
    def _():
        xn = (_rms(x_ref[0]) * g_ref[...]).astype(BF16)
        xn_ref[...] = xn
        if with_forget:
            fl = jnp.dot(xn, wf_ref[...], preferred_element_type=F32) + bf_ref[...]
            lf_ref[0] = jnp.minimum(fl, 0.0) - jnp.log(1.0 + jnp.exp(-jnp.abs(fl)))

    acc = jnp.dot(xn_ref[...], w_ref[...], preferred_element_type=F32)
    for n in range(acc.shape[1] // HEAD_DIM):
        a = acc[:, n * HEAD_DIM:(n + 1) * HEAD_DIM]
        if mode == "headnorm":
            a = _rms(a) * hg_ref[:, n * HEAD_DIM:(n + 1) * HEAD_DIM]
            if scale is not None:
                a = a * scale
        elif mode == "silu":
            a = a * _sigmoid(a)
        o_ref[0, n] = a.astype(o_ref.dtype)


def _norm_proj(h, g, w, *, mode, out_dtype, head_gain=None, scale=None, forget=None):
    bsz, lp, d = h.shape
    n_out = w.shape[1]
    tm = _pick_tile(lp, (1056, 704, 384, 128))
    tn = _pick_tile(n_out, (1024, 512, 256, 128))
    grid = (bsz, lp // tm, n_out // tn)
    in_specs = [
        pl.BlockSpec((1, tm, d), lambda b, i, j: (b, i, 0)),
        pl.BlockSpec((1, d), lambda b, i, j: (0, 0)),
        pl.BlockSpec((d, tn), lambda b, i, j: (0, j)),
    ]
    args = [h, g.reshape(1, d).astype(F32), w]
    if mode == "headnorm":
        in_specs.append(pl.BlockSpec((1, tn), lambda b, i, j: (0, j)))
        args.append(head_gain.reshape(1, n_out).astype(F32))
    out_shape = [jax.ShapeDtypeStruct((bsz, n_out // HEAD_DIM, lp, HEAD_DIM), out_dtype)]
    out_specs = [pl.BlockSpec((1, tn // HEAD_DIM, tm, HEAD_DIM), lambda b, i, j: (b, j, i, 0))]
    if forget is not None:
        w_f, b_f = forget
        in_specs += [pl.BlockSpec((d, LANES), lambda b, i, j: (0, 0)),
                     pl.BlockSpec((1, LANES), lambda b, i, j: (0, 0))]
        args += [w_f, b_f]
        out_shape.append(jax.ShapeDtypeStruct((bsz, lp, LANES), F32))
        out_specs.append(pl.BlockSpec((1, tm, LANES), lambda b, i, j: (b, i, 0)))
    outs = pl.pallas_call(
        functools.partial(_norm_proj_kernel, mode=mode, scale=scale, with_forget=forget is not None),
        grid=grid,
        in_specs=in_specs,
        out_specs=out_specs,
        out_shape=out_shape,
        scratch_shapes=[pltpu.VMEM((tm, d), BF16)],
        compiler_params=_params(("parallel", "parallel", "arbitrary")),
        name=f"norm_proj_{mode}",
    )(*args)
    return outs if forget is not None else outs[0]


def _matmul_residual_kernel(a_ref, w_ref, r_ref, o_ref):
    o_ref[...] = r_ref[...] + jnp.dot(a_ref[...], w_ref[...], preferred_element_type=F32)


def _matmul_residual(a, w, res):
    m, k = a.shape
    n = w.shape[1]
    tm = _pick_tile(m, (1056, 704, 384, 128))
    tn = _pick_tile(n, (1024, 512, 256, 128))
    return pl.pallas_call(
        _matmul_residual_kernel,
        grid=(m // tm, n // tn),
        in_specs=[
            pl.BlockSpec((tm, k), lambda i, j: (i, 0)),
            pl.BlockSpec((k, tn), lambda i, j: (0, j)),
            pl.BlockSpec((tm, tn), lambda i, j: (i, j)),
        ],
        out_specs=pl.BlockSpec((tm, tn), lambda i, j: (i, j)),
        out_shape=jax.ShapeDtypeStruct((m, n), F32),
        compiler_params=_params(("parallel", "parallel")),
        name="matmul_residual",
    )(a, w, res)


def _hgrn2_kernel(q_ref, f_ref, i_ref, z_ref, gam_ref, go_ref, o_ref, st_ref, *, rows_per_step):
    r = pl.program_id(2)

    @pl.when(r == 0)
    def _():
        st_ref[...] = jnp.zeros_like(st_ref)

    gam = gam_ref[...]
    e = jnp.exp(gam - jnp.max(gam, axis=0, keepdims=True))
    lb = e[0:1] / jnp.sum(e, axis=0, keepdims=True)
    g_out = go_ref[...]

    t_idx = lax.broadcasted_iota(jnp.int32, (CHUNK, CHUNK), 0)
    s_idx = lax.broadcasted_iota(jnp.int32, (CHUNK, CHUNK), 1)
    causal = s_idx <= t_idx
    tri = causal.astype(F32)
    row_in_chunk = lax.broadcasted_iota(jnp.int32, (CHUNK, 1), 0)
    nt_dims = (((1,), (1,)), ((), ()))
    tn_dims = (((0,), (0,)), ((), ()))

    def chunk_body(c, carry):
        start = pl.multiple_of(c * CHUNK, CHUNK)
        rows = pl.ds(start, CHUNK)
        q_raw = q_ref[0, 0, rows, :]
        f_raw = f_ref[0, 0, rows, :]
        v = i_ref[0, 0, rows, :]
        z = z_ref[0, 0, rows, :]
        valid = (r * rows_per_step + start + row_in_chunk) >= PAD_ROWS

        q = q_raw * _sigmoid(q_raw)
        fg = lb + (1.0 - lb) * _sigmoid(f_raw)
        logf = jnp.where(valid, jnp.log(fg), 0.0)
        k = jnp.where(valid, 1.0 - fg, 0.0)
        b = jnp.dot(tri, logf, precision=lax.Precision.HIGHEST, preferred_element_type=F32)
        b_mid = b[CHUNK // 2 - 1:CHUNK // 2, :]
        b_last = b[CHUNK - 1:CHUNK, :]

        q_intra = (q * jnp.exp(b - b_mid)).astype(BF16)
        k_intra = (k * jnp.exp(b_mid - b)).astype(BF16)
        a = lax.dot_general(q_intra, k_intra, nt_dims, preferred_element_type=F32)
        a = jnp.where(causal, a, 0.0).astype(BF16)
        v_b = v.astype(BF16)
        o_intra = jnp.dot(a, v_b, preferred_element_type=F32)

        st = st_ref[...]
        q_inter = (q * jnp.exp(b)).astype(BF16)
        o_inter = lax.dot_general(q_inter, st.astype(BF16), nt_dims, preferred_element_type=F32)
        o = o_intra + o_inter

        k_dec = (k * jnp.exp(b_last - b)).astype(BF16)
        d_st = lax.dot_general(v_b, k_dec, tn_dims, preferred_element_type=F32)
        st_ref[...] = jnp.exp(b_last) * st + d_st

        y = _rms(o) * g_out
        o_ref[0, rows, :] = (y * (z * _sigmoid(z))).astype(o_ref.dtype)
        return carry

    lax.fori_loop(0, rows_per_step // CHUNK, chunk_body, 0)


def _hgrn2(u, gamma_lb, g_out):
    bsz, four_h, lp, _ = u.shape
    n_heads = four_h // 4
    d = n_heads * HEAD_DIM
    rows = _pick_tile(lp, (704, 384, 128, 64))
    n_lb = gamma_lb.shape[0]

    def section(s):
        return pl.BlockSpec((1, 1, rows, HEAD_DIM), lambda b, h, r, s=s: (b, h + s * n_heads, r, 0))

    return pl.pallas_call(
        functools.partial(_hgrn2_kernel, rows_per_step=rows),
        grid=(bsz, n_heads, lp // rows),
        in_specs=[section(0), section(1), section(2), section(3),
                  pl.BlockSpec((n_lb, HEAD_DIM), lambda b, h, r: (0, h)),
                  pl.BlockSpec((1, HEAD_DIM), lambda b, h, r: (0, h))],
        out_specs=pl.BlockSpec((1, rows, HEAD_DIM), lambda b, h, r: (b, r, h)),
        out_shape=jax.ShapeDtypeStruct((bsz, lp, d), BF16),
        scratch_shapes=[pltpu.VMEM((HEAD_DIM, HEAD_DIM), F32)],
        compiler_params=_params(("parallel", "parallel", "arbitrary")),
        name="hgrn2_recurrence",
    )(u, u, u, u, gamma_lb.astype(F32), g_out.reshape(1, d).astype(F32))


def _forget_cumsum_kernel(lf_ref, frow_ref, fcol_ref, carry_ref, *, n_heads):
    i = pl.program_id(1)

    @pl.when(i == 0)
    def _():
        carry_ref[...] = jnp.zeros_like(carry_ref)

    t_idx = lax.broadcasted_iota(jnp.int32, (Q_BLOCK, Q_BLOCK), 0)
    s_idx = lax.broadcasted_iota(jnp.int32, (Q_BLOCK, Q_BLOCK), 1)
    tri = (s_idx <= t_idx).astype(F32)
    row = i * Q_BLOCK + lax.broadcasted_iota(jnp.int32, (Q_BLOCK, 1), 0)
    x = jnp.where(row >= PAD_ROWS, lf_ref[0], 0.0)
    c = jnp.dot(tri, x, precision=lax.Precision.HIGHEST, preferred_element_type=F32) + carry_ref[...]
    carry_ref[...] = c[Q_BLOCK - 1:Q_BLOCK, :]
    frow_ref[0] = c.T[:n_heads, :]
    for h in range(n_heads):
        fcol_ref[0, h] = jnp.broadcast_to(c[:, h:h + 1], (Q_BLOCK, LANES))


def _forget_cumsum(logf, n_heads):
    bsz, lp, _ = logf.shape
    return pl.pallas_call(
        functools.partial(_forget_cumsum_kernel, n_heads=n_heads),
        grid=(bsz, lp // Q_BLOCK),
        in_specs=[pl.BlockSpec((1, Q_BLOCK, LANES), lambda b, i: (b, i, 0))],
        out_specs=[pl.BlockSpec((1, n_heads, Q_BLOCK), lambda b, i: (b, 0, i)),
                   pl.BlockSpec((1, n_heads, Q_BLOCK, LANES), lambda b, i: (b, 0, i, 0))],
        out_shape=[jax.ShapeDtypeStruct((bsz, n_heads, lp), F32),
                   jax.ShapeDtypeStruct((bsz, n_heads, lp, LANES), F32)],
        scratch_shapes=[pltpu.VMEM((1, LANES), F32)],
        compiler_params=_params(("parallel", "arbitrary")),
        name="forget_cumsum",
    )(logf)


def _fox_attention_kernel(q_ref, k_ref, v_ref, fq_ref, fk_ref, gate_ref, o_ref, *, tile):
    qi = pl.program_id(2)
    q = q_ref[0, 0]
    fq = jnp.concatenate([fq_ref[0, 0]] * (tile // LANES), axis=1)
    nt_dims = (((1,), (1,)), ((), ()))
    col = lax.broadcasted_iota(jnp.int32, (1, tile), 1)

    def logits_for(j):
        start = pl.multiple_of(j * tile, tile)
        kt = k_ref[0, 0, pl.ds(start, tile), :]
        fk = fk_ref[0, 0, :, pl.ds(start, tile)]
        fk = jnp.where(start + col >= PAD_ROWS, fk, -MASK_VALUE)
        s = lax.dot_general(q, kt, nt_dims, preferred_element_type=F32)
        return s + fq - fk, start

    def update(carry, logits, start):
        m, l, acc = carry
        m_new = jnp.maximum(m, jnp.max(logits, axis=-1, keepdims=True))
        alpha = jnp.exp(m - m_new)
        p = jnp.exp(logits - m_new)
        vt = v_ref[0, 0, pl.ds(start, tile), :]
        acc = alpha * acc + jnp.dot(p.astype(BF16), vt, preferred_element_type=F32)
        return m_new, alpha * l + jnp.sum(p, axis=-1, keepdims=True), acc

    def below_diagonal(j, carry):
        logits, start = logits_for(j)
        return update(carry, logits, start)

    init = (jnp.full((tile, 1), -jnp.inf, F32), jnp.zeros((tile, 1), F32), jnp.zeros((tile, HEAD_DIM), F32))
    carry = lax.fori_loop(0, qi, below_diagonal, init)

    logits, start = logits_for(qi)
    row = lax.broadcasted_iota(jnp.int32, (tile, tile), 0)
    colf = lax.broadcasted_iota(jnp.int32, (tile, tile), 1)
    mask = (colf <= row) & (start + colf >= PAD_ROWS)
    logits = jnp.where(mask, logits, MASK_VALUE)
    _, l, acc = update(carry, logits, start)
    o_ref[0] = ((acc / l) * gate_ref[0, 0]).astype(o_ref.dtype)


def _fox_attention(q, k, v, fcol, frow, gate):
    bsz, n_heads, lp, _ = q.shape
    tile = _pick_tile(lp, (384, 128))
    tile_spec = pl.BlockSpec((1, 1, tile, HEAD_DIM), lambda b, h, i: (b, h, i, 0))
    full_spec = pl.BlockSpec((1, 1, lp, HEAD_DIM), lambda b, h, i: (b, h, 0, 0))
    return pl.pallas_call(
        functools.partial(_fox_attention_kernel, tile=tile),
        grid=(bsz, n_heads, lp // tile),
        in_specs=[tile_spec, full_spec, full_spec, tile_spec,
                  pl.BlockSpec((1, 1, 1, lp), lambda b, h, i: (b, h, 0, 0)),
                  tile_spec],
        out_specs=pl.BlockSpec((1, tile, HEAD_DIM), lambda b, h, i: (b, i, h)),
        out_shape=jax.ShapeDtypeStruct((bsz, lp, n_heads * HEAD_DIM), BF16),
        compiler_params=_params(("parallel", "parallel", "arbitrary")),
        name="fox_attention",
    )(q, k, v, fcol, frow, gate)


def kernel(x, meta, gamma_lb, a_norm, a_w_in, a_out_norm, a_w_out, kv_norm, kv_w, fox_b_f, fox_k_norm,
           b_norm, b_w_in, b_q_norm, b_w_out):
    bsz, seq, d = x.shape
    n_heads = d // HEAD_DIM
    n_a = a_norm.shape[0]
    n_b = b_norm.shape[0]
    lp = PAD_ROWS + N_META + seq

    h = jnp.concatenate([
        jnp.zeros((bsz, PAD_ROWS, d), x.dtype),
        jnp.broadcast_to(meta[None].astype(x.dtype), (bsz, N_META, d)),
        x], axis=1)

    for layer in range(n_a):
        assert layer == 0, "HGRN2 kernel computes the lower bound of layer 0"
        u = _norm_proj(h, a_norm[layer], a_w_in[layer].astype(BF16), mode="raw", out_dtype=F32)
        g = _hgrn2(u, gamma_lb, a_out_norm[layer])
        h = _matmul_residual(g.reshape(bsz * lp, d), a_w_out[layer].astype(BF16),
                             h.reshape(bsz * lp, d)).reshape(bsz, lp, d)

    w_f = jnp.zeros((d, LANES), BF16).at[:, :n_heads].set(kv_w[:, 2 * d:].astype(BF16))
    b_f = jnp.zeros((1, LANES), F32).at[0, :n_heads].set(fox_b_f.astype(F32))
    k_s, logf = _norm_proj(h, kv_norm, kv_w[:, :d].astype(BF16), mode="headnorm", out_dtype=BF16,
                           head_gain=fox_k_norm, forget=(w_f, b_f))
    v_s = _norm_proj(h, kv_norm, kv_w[:, d:2 * d].astype(BF16), mode="raw", out_dtype=BF16)
    frow, fcol = _forget_cumsum(logf, n_heads)
    frow = frow.reshape(bsz, n_heads, 1, lp)

    for j in range(n_b):
        q = _norm_proj(h, b_norm[j], b_w_in[j][:, :d].astype(BF16), mode="headnorm", out_dtype=BF16,
                       head_gain=b_q_norm[j], scale=HEAD_DIM ** -0.5)
        gate = _norm_proj(h, b_norm[j], b_w_in[j][:, d:].astype(BF16), mode="silu", out_dtype=F32)
        o = _fox_attention(q, k_s, v_s, fcol, frow, gate)
        h = _matmul_residual(o.reshape(bsz * lp, d), b_w_out[j].astype(BF16),
                             h.reshape(bsz * lp, d)).reshape(bsz, lp, d)

    return h[:, PAD_ROWS + N_META:]
```

```python
import functools

import jax
import jax.numpy as jnp
from jax import lax
from jax.experimental import pallas as pl
from jax.experimental.pallas import tpu as pltpu

F32 = jnp.float32
BF16 = jnp.bfloat16

HEAD_DIM = 128
N_META = 16
Q_BLOCK = 128
PAD_ROWS = Q_BLOCK - N_META
CHUNK = 64
EPS = 1e-6
MASK_VALUE = -1e30
LANES = 128
LOG2E = 1.4426950408889634
VMEM_LIMIT_BYTES = 60 * 1024 * 1024


def _pick_tile(n, candidates):
    for c in candidates:
        if n % c == 0:
            return c
    raise ValueError(f"no tile in {candidates} divides {n}")


def _params(semantics):
    return pltpu.CompilerParams(dimension_semantics=semantics, vmem_limit_bytes=VMEM_LIMIT_BYTES)


def _sigmoid(x):
    return 1.0 / (1.0 + jnp.exp(-x))


def _rms(x):
    return x * lax.rsqrt(jnp.mean(x * x, axis=-1, keepdims=True) + EPS)


def _norm_proj_kernel(*refs, mode, scale, with_forget):
    if with_forget:
        x_ref, g_ref, w_ref, hg_ref, wf_ref, bf_ref, o_ref, lf_ref, xn_ref = refs
    elif mode == "headnorm":
        x_ref, g_ref, w_ref, hg_ref, o_ref, xn_ref = refs
    else:
        x_ref, g_ref, w_ref, o_ref, xn_ref = refs

    @pl.when(pl.program_id(2) == 0)
    def _():
        xn = (_rms(x_ref[0]) * g_ref[...]).astype(BF16)
        xn_ref[...] = xn
        if with_forget:
            fl = jnp.dot(xn, wf_ref[...], preferred_element_type=F32) + bf_ref[...]
            lf_ref[0] = jnp.minimum(fl, 0.0) - jnp.log(1.0 + jnp.exp(-jnp.abs(fl)))

    acc = jnp.dot(xn_ref[...], w_ref[...], preferred_element_type=F32)
    for n in range(acc.shape[1] // HEAD_DIM):
        a = acc[:, n * HEAD_DIM:(n + 1) * HEAD_DIM]
        if mode == "headnorm":
            a = _rms(a) * hg_ref[:, n * HEAD_DIM:(n + 1) * HEAD_DIM]
            if scale is not None:
                a = a * scale
        elif mode == "silu":
            a = a * _sigmoid(a)
        o_ref[0, n] = a.astype(o_ref.dtype)


def _norm_proj(h, g, w, *, mode, out_dtype, head_gain=None, scale=None, forget=None):
    bsz, lp, d = h.shape
    n_out = w.shape[1]
    tm = _pick_tile(lp, (1056, 704, 384, 128))
    tn = _pick_tile(n_out, (1024, 512, 256, 128))
    grid = (bsz, lp // tm, n_out // tn)
    in_specs = [
        pl.BlockSpec((1, tm, d), lambda b, i, j: (b, i, 0)),
        pl.BlockSpec((1, d), lambda b, i, j: (0, 0)),
        pl.BlockSpec((d, tn), lambda b, i, j: (0, j)),
    ]
    args = [h, g.reshape(1, d).astype(F32), w]
    if mode == "headnorm":
        in_specs.append(pl.BlockSpec((1, tn), lambda b, i, j: (0, j)))
        args.append(head_gain.reshape(1, n_out).astype(F32))
    out_shape = [jax.ShapeDtypeStruct((bsz, n_out // HEAD_DIM, lp, HEAD_DIM), out_dtype)]
    out_specs = [pl.BlockSpec((1, tn // HEAD_DIM, tm, HEAD_DIM), lambda b, i, j: (b, j, i, 0))]
    if forget is not None:
        w_f, b_f = forget
        in_specs += [pl.BlockSpec((d, LANES), lambda b, i, j: (0, 0)),
                     pl.BlockSpec((1, LANES), lambda b, i, j: (0, 0))]
        args += [w_f, b_f]
        out_shape.append(jax.ShapeDtypeStruct((bsz, lp, LANES), F32))
        out_specs.append(pl.BlockSpec((1, tm, LANES), lambda b, i, j: (b, i, 0)))
    outs = pl.pallas_call(
        functools.partial(_norm_proj_kernel, mode=mode, scale=scale, with_forget=forget is not None),
        grid=grid,
        in_specs=in_specs,
        out_specs=out_specs,
        out_shape=out_shape,
        scratch_shapes=[pltpu.VMEM((tm, d), BF16)],
        compiler_params=_params(("parallel", "parallel", "arbitrary")),
        name=f"norm_proj_{mode}",
    )(*args)
    return outs if forget is not None else outs[0]


def _matmul_residual_kernel(a_ref, w_ref, r_ref, o_ref):
    o_ref[...] = r_ref[...] + jnp.dot(a_ref[...], w_ref[...], preferred_element_type=F32)


def _matmul_residual(a, w, res):
    m, k = a.shape
    n = w.shape[1]
    tm = _pick_tile(m, (1056, 704, 384, 128))
    tn = _pick_tile(n, (1024, 512, 256, 128))
    return pl.pallas_call(
        _matmul_residual_kernel,
        grid=(m // tm, n // tn),
        in_specs=[
            pl.BlockSpec((tm, k), lambda i, j: (i, 0)),
            pl.BlockSpec((k, tn), lambda i, j: (0, j)),
            pl.BlockSpec((tm, tn), lambda i, j: (i, j)),
        ],
        out_specs=pl.BlockSpec((tm, tn), lambda i, j: (i, j)),
        out_shape=jax.ShapeDtypeStruct((m, n), F32),
        compiler_params=_params(("parallel", "parallel")),
        name="matmul_residual",
    )(a, w, res)


def _hgrn2_kernel(q_ref, f_ref, i_ref, z_ref, gam_ref, go_ref, o_ref, st_ref, *, rows_per_step):
    r = pl.program_id(2)

    @pl.when(r == 0)
    def _():
        st_ref[...] = jnp.zeros_like(st_ref)

    gam = gam_ref[...]
    e = jnp.exp(gam - jnp.max(gam, axis=0, keepdims=True))
    lb = e[0:1] / jnp.sum(e, axis=0, keepdims=True)
    g_out = go_ref[...]

    t_idx = lax.broadcasted_iota(jnp.int32, (CHUNK, CHUNK), 0)
    s_idx = lax.broadcasted_iota(jnp.int32, (CHUNK, CHUNK), 1)
    causal = s_idx <= t_idx
    tri = causal.astype(F32)
    nt_dims = (((1,), (1,)), ((), ()))
    tn_dims = (((0,), (0,)), ((), ()))
    n_chunks = rows_per_step // CHUNK

    def chunk(a, c):
        return a[c * CHUNK:(c + 1) * CHUNK]

    q_raw = q_ref[0, 0]
    f_raw = f_ref[0, 0]
    row = r * rows_per_step + lax.broadcasted_iota(jnp.int32, (rows_per_step, 1), 0)
    valid = row >= PAD_ROWS
    q = q_raw * _sigmoid(q_raw)
    fg = lb + (1.0 - lb) * _sigmoid(f_raw)
    logf = jnp.where(valid, jnp.log(fg), 0.0)
    k = jnp.where(valid, 1.0 - fg, 0.0)
    v_b = i_ref[0, 0].astype(BF16)

    b = [jnp.dot(tri, chunk(logf, c), precision=lax.Precision.HIGHEST, preferred_element_type=F32)
         for c in range(n_chunks)]

    q_intra, k_intra, q_inter, k_dec, decay = [], [], [], [], []
    for c in range(n_chunks):
        b_mid = b[c][CHUNK // 2 - 1:CHUNK // 2, :]
        b_last = b[c][CHUNK - 1:CHUNK, :]
        q_c, k_c = chunk(q, c), chunk(k, c)
        q_intra.append((q_c * jnp.exp(b[c] - b_mid)).astype(BF16))
        k_intra.append((k_c * jnp.exp(b_mid - b[c])).astype(BF16))
        q_inter.append((q_c * jnp.exp(b[c])).astype(BF16))
        k_dec.append((k_c * jnp.exp(b_last - b[c])).astype(BF16))
        decay.append(jnp.exp(b_last))

    a = [lax.dot_general(q_intra[c], k_intra[c], nt_dims, preferred_element_type=F32) for c in range(n_chunks)]
    a = [jnp.where(causal, a_c, 0.0).astype(BF16) for a_c in a]
    o_intra = [jnp.dot(a[c], chunk(v_b, c), preferred_element_type=F32) for c in range(n_chunks)]
    d_st = [lax.dot_general(chunk(v_b, c), k_dec[c], tn_dims, preferred_element_type=F32)
            for c in range(n_chunks)]

    st = st_ref[...]
    st_before = []
    for c in range(n_chunks):
        st_before.append(st.astype(BF16))
        st = decay[c] * st + d_st[c]
    st_ref[...] = st

    o = jnp.concatenate(
        [o_intra[c] + lax.dot_general(q_inter[c], st_before[c], nt_dims, preferred_element_type=F32)
         for c in range(n_chunks)], axis=0)
    z = z_ref[0, 0]
    o_ref[0] = (_rms(o) * g_out * (z * _sigmoid(z))).astype(o_ref.dtype)


def _hgrn2(u, gamma_lb, g_out):
    bsz, four_h, lp, _ = u.shape
    n_heads = four_h // 4
    d = n_heads * HEAD_DIM
    rows = _pick_tile(lp, (704, 384, 128, 64))
    n_lb = gamma_lb.shape[0]

    def section(s):
        return pl.BlockSpec((1, 1, rows, HEAD_DIM), lambda b, h, r, s=s: (b, h + s * n_heads, r, 0))

    return pl.pallas_call(
        functools.partial(_hgrn2_kernel, rows_per_step=rows),
        grid=(bsz, n_heads, lp // rows),
        in_specs=[section(0), section(1), section(2), section(3),
                  pl.BlockSpec((n_lb, HEAD_DIM), lambda b, h, r: (0, h)),
                  pl.BlockSpec((1, HEAD_DIM), lambda b, h, r: (0, h))],
        out_specs=pl.BlockSpec((1, rows, HEAD_DIM), lambda b, h, r: (b, r, h)),
        out_shape=jax.ShapeDtypeStruct((bsz, lp, d), BF16),
        scratch_shapes=[pltpu.VMEM((HEAD_DIM, HEAD_DIM), F32)],
        compiler_params=_params(("parallel", "parallel", "arbitrary")),
        name="hgrn2_recurrence",
    )(u, u, u, u, gamma_lb.astype(F32), g_out.reshape(1, d).astype(F32))


N_F_PIECES = 3


def _forget_features_kernel(lf_ref, qx_ref, kx_ref, carry_ref, *, n_heads):
    i = pl.program_id(1)

    @pl.when(i == 0)
    def _():
        carry_ref[...] = jnp.zeros_like(carry_ref)

    t_idx = lax.broadcasted_iota(jnp.int32, (Q_BLOCK, Q_BLOCK), 0)
    s_idx = lax.broadcasted_iota(jnp.int32, (Q_BLOCK, Q_BLOCK), 1)
    tri = (s_idx <= t_idx).astype(F32)
    row = i * Q_BLOCK + lax.broadcasted_iota(jnp.int32, (Q_BLOCK, 1), 0)
    real = row >= PAD_ROWS
    x = jnp.where(real, lf_ref[0], 0.0)
    c = jnp.dot(tri, x, precision=lax.Precision.HIGHEST, preferred_element_type=F32) + carry_ref[...]
    carry_ref[...] = c[Q_BLOCK - 1:Q_BLOCK, :]

    lane = lax.broadcasted_iota(jnp.int32, (Q_BLOCK, LANES), 1)
    for h in range(n_heads):
        f = c[:, h:h + 1] * LOG2E
        pieces = []
        for _ in range(N_F_PIECES):
            piece = f.astype(BF16).astype(F32)
            pieces.append(piece)
            f = f - piece
        qx = jnp.where(lane < 2 * N_F_PIECES, 1.0, 0.0)
        kx = qx
        for n, piece in enumerate(pieces):
            qx = jnp.where(lane == n, piece, qx)
            kx = jnp.where(lane == N_F_PIECES + n, -piece, kx)
        kx = jnp.where(real, kx, jnp.where(lane == N_F_PIECES, MASK_VALUE, 0.0))
        qx_ref[0, h] = qx.astype(BF16)
        kx_ref[0, h] = kx.astype(BF16)


def _forget_features(logf, n_heads):
    bsz, lp, _ = logf.shape
    out = jax.ShapeDtypeStruct((bsz, n_heads, lp, LANES), BF16)
    spec = pl.BlockSpec((1, n_heads, Q_BLOCK, LANES), lambda b, i: (b, 0, i, 0))
    return pl.pallas_call(
        functools.partial(_forget_features_kernel, n_heads=n_heads),
        grid=(bsz, lp // Q_BLOCK),
        in_specs=[pl.BlockSpec((1, Q_BLOCK, LANES), lambda b, i: (b, i, 0))],
        out_specs=[spec, spec],
        out_shape=[out, out],
        scratch_shapes=[pltpu.VMEM((1, LANES), F32)],
        compiler_params=_params(("parallel", "arbitrary")),
        name="forget_features",
    )(logf)


SCORE_LOOKAHEAD = 2


def _fox_attention_kernel(q_ref, qx_ref, k_ref, kx_ref, v_ref, gate_ref, o_ref, *, tile):
    nt_dims = (((1,), (1,)), ((), ()))
    n_tiles = q_ref.shape[2] // tile
    row = lax.broadcasted_iota(jnp.int32, (tile, tile), 0)
    col = lax.broadcasted_iota(jnp.int32, (tile, tile), 1)
    causal = col <= row

    def rows(t):
        return pl.ds(t * tile, tile)

    def update(carry, s, j):
        m_old, l, acc = carry
        blocks = [s[:, c * LANES:(c + 1) * LANES] for c in range(tile // LANES)]
        row_max = jnp.max(functools.reduce(jnp.maximum, blocks), axis=-1, keepdims=True)
        m_new = jnp.maximum(m_old, row_max)
        alpha = jnp.exp2(m_old - m_new)
        p_blocks = [jnp.exp2(blk - m_new) for blk in blocks]
        l = alpha * l + functools.reduce(jnp.add, p_blocks)
        p = jnp.concatenate(p_blocks, axis=1).astype(BF16)
        acc = alpha * acc + jnp.dot(p, v_ref[0, 0, rows(j), :], preferred_element_type=F32)
        return m_new, l, acc

    def scores(qi, j):
        q = jnp.concatenate([q_ref[0, 0, rows(qi), :], qx_ref[0, 0, rows(qi), :]], axis=1)
        kt = jnp.concatenate([k_ref[0, 0, rows(j), :], kx_ref[0, 0, rows(j), :]], axis=1)
        s = lax.dot_general(q, kt, nt_dims, preferred_element_type=F32)
        return jnp.where(causal, s, MASK_VALUE) if j == qi else s

    pairs = [(qi, j) for qi in range(n_tiles) for j in range(qi + 1)]
    pending = [scores(*pair) for pair in pairs[:SCORE_LOOKAHEAD]]
    carry = None
    for n, (qi, j) in enumerate(pairs):
        if n + SCORE_LOOKAHEAD < len(pairs):
            pending.append(scores(*pairs[n + SCORE_LOOKAHEAD]))
        if j == 0:
            carry = (jnp.full((tile, LANES), -jnp.inf, F32), jnp.zeros((tile, LANES), F32),
                     jnp.zeros((tile, HEAD_DIM), F32))
        carry = update(carry, pending.pop(0), j)
        if j == qi:
            _, l, acc = carry
            l = jnp.sum(l, axis=-1, keepdims=True)
            o_ref[0, rows(qi), :] = ((acc / l) * gate_ref[0, 0, rows(qi), :]).astype(o_ref.dtype)


def _fox_attention(q, qx, k, kx, v, gate):
    bsz, n_heads, lp, _ = q.shape
    tile = _pick_tile(lp, (384, 128))
    spec = pl.BlockSpec((1, 1, lp, HEAD_DIM), lambda b, h: (b, h, 0, 0))
    return pl.pallas_call(
        functools.partial(_fox_attention_kernel, tile=tile),
        grid=(bsz, n_heads),
        in_specs=[spec] * 6,
        out_specs=pl.BlockSpec((1, lp, HEAD_DIM), lambda b, h: (b, 0, h)),
        out_shape=jax.ShapeDtypeStruct((bsz, lp, n_heads * HEAD_DIM), BF16),
        compiler_params=_params(("parallel", "parallel")),
        name="fox_attention",
    )(q, qx, k, kx, v, gate)


def kernel(x, meta, gamma_lb, a_norm, a_w_in, a_out_norm, a_w_out, kv_norm, kv_w, fox_b_f, fox_k_norm,
           b_norm, b_w_in, b_q_norm, b_w_out):
    bsz, seq, d = x.shape
    n_heads = d // HEAD_DIM
    n_a = a_norm.shape[0]
    n_b = b_norm.shape[0]
    lp = PAD_ROWS + N_META + seq

    h = jnp.concatenate([
        jnp.zeros((bsz, PAD_ROWS, d), x.dtype),
        jnp.broadcast_to(meta[None].astype(x.dtype), (bsz, N_META, d)),
        x], axis=1)

    for layer in range(n_a):
        assert layer == 0, "HGRN2 kernel computes the lower bound of layer 0"
        u = _norm_proj(h, a_norm[layer], a_w_in[layer].astype(BF16), mode="raw", out_dtype=F32)
        g = _hgrn2(u, gamma_lb, a_out_norm[layer])
        h = _matmul_residual(g.reshape(bsz * lp, d), a_w_out[layer].astype(BF16),
                             h.reshape(bsz * lp, d)).reshape(bsz, lp, d)

    w_f = jnp.zeros((d, LANES), BF16).at[:, :n_heads].set(kv_w[:, 2 * d:].astype(BF16))
    b_f = jnp.zeros((1, LANES), F32).at[0, :n_heads].set(fox_b_f.astype(F32))
    k_s, logf = _norm_proj(h, kv_norm, kv_w[:, :d].astype(BF16), mode="headnorm", out_dtype=BF16,
                           head_gain=fox_k_norm, forget=(w_f, b_f))
    v_s = _norm_proj(h, kv_norm, kv_w[:, d:2 * d].astype(BF16), mode="raw", out_dtype=BF16)
    qx, kx = _forget_features(logf, n_heads)

    for j in range(n_b):
        q = _norm_proj(h, b_norm[j], b_w_in[j][:, :d].astype(BF16), mode="headnorm", out_dtype=BF16,
                       head_gain=b_q_norm[j], scale=HEAD_DIM ** -0.5 * LOG2E)
        gate = _norm_proj(h, b_norm[j], b_w_in[j][:, d:].astype(BF16), mode="silu", out_dtype=F32)
        o = _fox_attention(q, qx, k_s, kx, v_s, gate)
        h = _matmul_residual(o.reshape(bsz * lp, d), b_w_out[j].astype(BF16),
                             h.reshape(bsz * lp, d)).reshape(bsz, lp, d)

    return h[:, PAD_ROWS + N_META:]
```

```python
import functools

import jax
import jax.numpy as jnp
from jax import lax
from jax.experimental import pallas as pl
from jax.experimental.pallas import tpu as pltpu

F32 = jnp.float32
BF16 = jnp.bfloat16

HEAD_DIM = 128
N_META = 16
Q_BLOCK = 128
PAD_ROWS = Q_BLOCK - N_META
HEAD_ROWS = PAD_ROWS + N_META
CHUNK = 64
EPS = 1e-6
MASK_VALUE = -1e30
LANES = 128
LOG2E = 1.4426950408889634
VMEM_LIMIT_BYTES = 60 * 1024 * 1024


def _pick_tile(n, candidates):
    for c in candidates:
        if n % c == 0:
            return c
    raise ValueError(f"no tile in {candidates} divides {n}")


def _params(semantics):
    return pltpu.CompilerParams(dimension_semantics=semantics, vmem_limit_bytes=VMEM_LIMIT_BYTES)


def _sigmoid(x):
    return 1.0 / (1.0 + jnp.exp(-x))


def _rms(x):
    return x * lax.rsqrt(jnp.mean(x * x, axis=-1, keepdims=True) + EPS)


def _norm_proj_kernel(x_ref, g_ref, w_ref, o_ref, xn_ref):
    @pl.when(pl.program_id(2) == 0)
    def _():
        xn_ref[...] = (_rms(x_ref[0]) * g_ref[...]).astype(BF16)

    acc = jnp.dot(xn_ref[...], w_ref[...], preferred_element_type=F32)
    for n in range(acc.shape[1] // HEAD_DIM):
        o_ref[0, n] = acc[:, n * HEAD_DIM:(n + 1) * HEAD_DIM]


def _norm_proj(h, g, w):
    bsz, lp, d = h.shape
    n_out = w.shape[1]
    tm = _pick_tile(lp, (1056, 704, 384, 128))
    tn = _pick_tile(n_out, (1024, 512, 256, 128))
    return pl.pallas_call(
        _norm_proj_kernel,
        grid=(bsz, lp // tm, n_out // tn),
        in_specs=[
            pl.BlockSpec((1, tm, d), lambda b, i, j: (b, i, 0)),
            pl.BlockSpec((1, d), lambda b, i, j: (0, 0)),
            pl.BlockSpec((d, tn), lambda b, i, j: (0, j)),
        ],
        out_specs=pl.BlockSpec((1, tn // HEAD_DIM, tm, HEAD_DIM), lambda b, i, j: (b, j, i, 0)),
        out_shape=jax.ShapeDtypeStruct((bsz, n_out // HEAD_DIM, lp, HEAD_DIM), F32),
        scratch_shapes=[pltpu.VMEM((tm, d), BF16)],
        compiler_params=_params(("parallel", "parallel", "arbitrary")),
        name="norm_proj",
    )(h, g.reshape(1, d).astype(F32), w)


def _hgrn2_kernel(q_ref, f_ref, i_ref, z_ref, gam_ref, go_ref, o_ref, st_ref, *, rows_per_step):
    r = pl.program_id(2)

    @pl.when(r == 0)
    def _():
        st_ref[...] = jnp.zeros_like(st_ref)

    gam = gam_ref[...]
    e = jnp.exp(gam - jnp.max(gam, axis=0, keepdims=True))
    lb = e[0:1] / jnp.sum(e, axis=0, keepdims=True)
    g_out = go_ref[...]

    t_idx = lax.broadcasted_iota(jnp.int32, (CHUNK, CHUNK), 0)
    s_idx = lax.broadcasted_iota(jnp.int32, (CHUNK, CHUNK), 1)
    causal = s_idx <= t_idx
    tri = causal.astype(F32)
    nt_dims = (((1,), (1,)), ((), ()))
    tn_dims = (((0,), (0,)), ((), ()))
    n_chunks = rows_per_step // CHUNK

    def chunk(a, c):
        return a[c * CHUNK:(c + 1) * CHUNK]

    q_raw = q_ref[0, 0]
    f_raw = f_ref[0, 0]
    row = r * rows_per_step + lax.broadcasted_iota(jnp.int32, (rows_per_step, 1), 0)
    valid = row >= PAD_ROWS
    q = q_raw * _sigmoid(q_raw)
    fg = lb + (1.0 - lb) * _sigmoid(f_raw)
    logf = jnp.where(valid, jnp.log(fg), 0.0)
    k = jnp.where(valid, 1.0 - fg, 0.0)
    v_b = i_ref[0, 0].astype(BF16)

    b = [jnp.dot(tri, chunk(logf, c), precision=lax.Precision.HIGHEST, preferred_element_type=F32)
         for c in range(n_chunks)]

    q_intra, k_intra, q_inter, k_dec, decay = [], [], [], [], []
    for c in range(n_chunks):
        b_mid = b[c][CHUNK // 2 - 1:CHUNK // 2, :]
        b_last = b[c][CHUNK - 1:CHUNK, :]
        q_c, k_c = chunk(q, c), chunk(k, c)
        q_intra.append((q_c * jnp.exp(b[c] - b_mid)).astype(BF16))
        k_intra.append((k_c * jnp.exp(b_mid - b[c])).astype(BF16))
        q_inter.append((q_c * jnp.exp(b[c])).astype(BF16))
        k_dec.append((k_c * jnp.exp(b_last - b[c])).astype(BF16))
        decay.append(jnp.exp(b_last))

    a = [lax.dot_general(q_intra[c], k_intra[c], nt_dims, preferred_element_type=F32) for c in range(n_chunks)]
    a = [jnp.where(causal, a_c, 0.0).astype(BF16) for a_c in a]
    o_intra = [jnp.dot(a[c], chunk(v_b, c), preferred_element_type=F32) for c in range(n_chunks)]
    d_st = [lax.dot_general(chunk(v_b, c), k_dec[c], tn_dims, preferred_element_type=F32)
            for c in range(n_chunks)]

    st = st_ref[...]
    st_before = []
    for c in range(n_chunks):
        st_before.append(st.astype(BF16))
        st = decay[c] * st + d_st[c]
    st_ref[...] = st

    o = jnp.concatenate(
        [o_intra[c] + lax.dot_general(q_inter[c], st_before[c], nt_dims, preferred_element_type=F32)
         for c in range(n_chunks)], axis=0)
    z = z_ref[0, 0]
    o_ref[0] = (_rms(o) * g_out * (z * _sigmoid(z))).astype(o_ref.dtype)


def _hgrn2(u, gamma_lb, g_out):
    bsz, four_h, lp, _ = u.shape
    n_heads = four_h // 4
    d = n_heads * HEAD_DIM
    rows = _pick_tile(lp, (704, 384, 128, 64))
    n_lb = gamma_lb.shape[0]

    def section(s):
        return pl.BlockSpec((1, 1, rows, HEAD_DIM), lambda b, h, r, s=s: (b, h + s * n_heads, r, 0))

    return pl.pallas_call(
        functools.partial(_hgrn2_kernel, rows_per_step=rows),
        grid=(bsz, n_heads, lp // rows),
        in_specs=[section(0), section(1), section(2), section(3),
                  pl.BlockSpec((n_lb, HEAD_DIM), lambda b, h, r: (0, h)),
                  pl.BlockSpec((1, HEAD_DIM), lambda b, h, r: (0, h))],
        out_specs=pl.BlockSpec((1, rows, HEAD_DIM), lambda b, h, r: (b, r, h)),
        out_shape=jax.ShapeDtypeStruct((bsz, lp, d), BF16),
        scratch_shapes=[pltpu.VMEM((HEAD_DIM, HEAD_DIM), F32)],
        compiler_params=_params(("parallel", "parallel", "arbitrary")),
        name="hgrn2_recurrence",
    )(u, u, u, u, gamma_lb.astype(F32), g_out.reshape(1, d).astype(F32))


def _out_proj_norms_kernel(a_ref, w_ref, r_ref, gkv_ref, gb_ref, wf_ref, bf_ref,
                           h_ref, xkv_ref, xb_ref, lf_ref):
    h = r_ref[...] + jnp.dot(a_ref[...], w_ref[...], preferred_element_type=F32)
    h_ref[...] = h
    y = _rms(h)
    x_kv = (y * gkv_ref[...]).astype(BF16)
    xkv_ref[...] = x_kv
    xb_ref[...] = (y * gb_ref[...]).astype(BF16)
    fl = jnp.dot(x_kv, wf_ref[...], preferred_element_type=F32) + bf_ref[...]
    lf_ref[...] = jnp.minimum(fl, 0.0) - jnp.log(1.0 + jnp.exp(-jnp.abs(fl)))


def _out_proj_norms(a, w, res, g_kv, g_b, w_f, b_f):
    m, d = a.shape
    tm = _pick_tile(m, (528, 384, 256, 128))
    row_spec = pl.BlockSpec((tm, d), lambda i: (i, 0))
    vec_spec = pl.BlockSpec((1, d), lambda i: (0, 0))
    return pl.pallas_call(
        _out_proj_norms_kernel,
        grid=(m // tm,),
        in_specs=[row_spec, pl.BlockSpec((d, d), lambda i: (0, 0)), row_spec, vec_spec, vec_spec,
                  pl.BlockSpec((d, LANES), lambda i: (0, 0)), pl.BlockSpec((1, LANES), lambda i: (0, 0))],
        out_specs=[row_spec, row_spec, row_spec, pl.BlockSpec((tm, LANES), lambda i: (i, 0))],
        out_shape=[jax.ShapeDtypeStruct((m, d), F32), jax.ShapeDtypeStruct((m, d), BF16),
                   jax.ShapeDtypeStruct((m, d), BF16), jax.ShapeDtypeStruct((m, LANES), F32)],
        compiler_params=_params(("parallel",)),
        name="out_proj_norms",
    )(a, w, res, g_kv.reshape(1, d).astype(F32), g_b.reshape(1, d).astype(F32), w_f, b_f)


PROJ_COLS_PER_DOT = 512


def _head_proj_kernel(*refs, mode, scale):
    if mode == "headnorm":
        x_ref, w_ref, hg_ref, o_ref = refs
    else:
        x_ref, w_ref, o_ref = refs
    x = x_ref[0]
    n_out = w_ref.shape[1]
    cols_per_dot = min(PROJ_COLS_PER_DOT, n_out)
    assert n_out % cols_per_dot == 0
    heads_per_dot = cols_per_dot // HEAD_DIM
    for c in range(n_out // cols_per_dot):
        cols = pl.ds(c * cols_per_dot, cols_per_dot)
        acc = jnp.dot(x, w_ref[:, cols], preferred_element_type=F32)
        for n in range(heads_per_dot):
            a = acc[:, n * HEAD_DIM:(n + 1) * HEAD_DIM]
            head = c * heads_per_dot + n
            if mode == "headnorm":
                a = _rms(a) * hg_ref[:, head * HEAD_DIM:(head + 1) * HEAD_DIM]
                if scale is not None:
                    a = a * scale
            elif mode == "silu":
                a = a * _sigmoid(a)
            o_ref[0, head] = a.astype(o_ref.dtype)


def _head_proj(x, w, *, mode, out_dtype, head_gain=None, scale=None):
    bsz, lp, d = x.shape
    n_out = w.shape[1]
    tm = _pick_tile(lp, (1056, 704, 384, 128))
    in_specs = [pl.BlockSpec((1, tm, d), lambda b, i: (b, i, 0)),
                pl.BlockSpec((d, n_out), lambda b, i: (0, 0))]
    args = [x, w]
    if mode == "headnorm":
        in_specs.append(pl.BlockSpec((1, n_out), lambda b, i: (0, 0)))
        args.append(head_gain.reshape(1, n_out).astype(F32))
    return pl.pallas_call(
        functools.partial(_head_proj_kernel, mode=mode, scale=scale),
        grid=(bsz, lp // tm),
        in_specs=in_specs,
        out_specs=pl.BlockSpec((1, n_out // HEAD_DIM, tm, HEAD_DIM), lambda b, i: (b, 0, i, 0)),
        out_shape=jax.ShapeDtypeStruct((bsz, n_out // HEAD_DIM, lp, HEAD_DIM), out_dtype),
        compiler_params=_params(("parallel", "parallel")),
        name=f"head_proj_{mode}",
    )(*args)


N_F_PIECES = 3


def _feature_scatter(n_heads):
    r = jnp.arange((N_F_PIECES + 1) * LANES)[:, None]
    c = jnp.arange(n_heads * LANES)[None, :]
    piece, src = r // LANES, r % LANES
    head, lane = c // LANES, c % LANES
    is_piece = (piece < N_F_PIECES) & (src == head)
    is_one = (piece == N_F_PIECES) & (src == 0)
    sel_q = (is_piece & (lane == piece)) | (is_one & (lane >= N_F_PIECES) & (lane < 2 * N_F_PIECES))
    sel_k = (is_one & (lane < N_F_PIECES)).astype(F32) - (is_piece & (lane == N_F_PIECES + piece)).astype(F32)
    return sel_q.astype(BF16), sel_k.astype(BF16)


def _forget_features_kernel(lf_ref, selq_ref, selk_ref, qx_ref, kx_ref, carry_ref, *, n_heads, block_rows):
    i = pl.program_id(1)

    @pl.when(i == 0)
    def _():
        carry_ref[...] = jnp.zeros_like(carry_ref)

    t_idx = lax.broadcasted_iota(jnp.int32, (Q_BLOCK, Q_BLOCK), 0)
    s_idx = lax.broadcasted_iota(jnp.int32, (Q_BLOCK, Q_BLOCK), 1)
    tri = (s_idx <= t_idx).astype(F32)
    lane = lax.broadcasted_iota(jnp.int32, (Q_BLOCK, LANES), 1)
    pad_key = jnp.where(lane == N_F_PIECES, MASK_VALUE, 0.0)
    ones = jnp.ones((Q_BLOCK, LANES), BF16)
    carry = carry_ref[...]
    for sub in range(block_rows // Q_BLOCK):
        rows = pl.ds(sub * Q_BLOCK, Q_BLOCK)
        row = i * block_rows + sub * Q_BLOCK + lax.broadcasted_iota(jnp.int32, (Q_BLOCK, 1), 0)
        real = row >= PAD_ROWS
        x = jnp.where(real, lf_ref[0, rows, :], 0.0)
        c = jnp.dot(tri, x, precision=lax.Precision.HIGHEST, preferred_element_type=F32) + carry
        carry = c[Q_BLOCK - 1:Q_BLOCK, :]
        f = c * LOG2E
        pieces = []
        for _ in range(N_F_PIECES):
            piece = f.astype(BF16)
            pieces.append(piece)
            f = f - piece.astype(F32)
        stacked = jnp.concatenate(pieces + [ones], axis=1)
        qx = jnp.dot(stacked, selq_ref[...], preferred_element_type=F32)
        kx = jnp.dot(stacked, selk_ref[...], preferred_element_type=F32)
        for h in range(n_heads):
            head = slice(h * LANES, (h + 1) * LANES)
            qx_ref[0, h, rows, :] = qx[:, head].astype(BF16)
            kx_ref[0, h, rows, :] = jnp.where(real, kx[:, head], pad_key).astype(BF16)
    carry_ref[...] = carry


def _forget_features(logf, n_heads):
    bsz, lp, _ = logf.shape
    block_rows = _pick_tile(lp, (384, 128))
    sel_q, sel_k = _feature_scatter(n_heads)
    out = jax.ShapeDtypeStruct((bsz, n_heads, lp, LANES), BF16)
    spec = pl.BlockSpec((1, n_heads, block_rows, LANES), lambda b, i: (b, 0, i, 0))
    sel_spec = pl.BlockSpec(sel_q.shape, lambda b, i: (0, 0))
    return pl.pallas_call(
        functools.partial(_forget_features_kernel, n_heads=n_heads, block_rows=block_rows),
        grid=(bsz, lp // block_rows),
        in_specs=[pl.BlockSpec((1, block_rows, LANES), lambda b, i: (b, i, 0)), sel_spec, sel_spec],
        out_specs=[spec, spec],
        out_shape=[out, out],
        scratch_shapes=[pltpu.VMEM((1, LANES), F32)],
        compiler_params=_params(("parallel", "arbitrary")),
        name="forget_features",
    )(logf, sel_q, sel_k)


SCORE_LOOKAHEAD = 2


def _fox_attention_kernel(q_ref, qx_ref, k_ref, kx_ref, v_ref, gate_ref, o_ref, *, tile):
    nt_dims = (((1,), (1,)), ((), ()))
    n_tiles = q_ref.shape[2] // tile
    row = lax.broadcasted_iota(jnp.int32, (tile, tile), 0)
    col = lax.broadcasted_iota(jnp.int32, (tile, tile), 1)
    causal = col <= row

    def rows(t):
        return pl.ds(t * tile, tile)

    def update(carry, s, j):
        m_old, l, acc = carry
        blocks = [s[:, c * LANES:(c + 1) * LANES] for c in range(tile // LANES)]
        row_max = jnp.max(functools.reduce(jnp.maximum, blocks), axis=-1, keepdims=True)
        m_new = jnp.maximum(m_old, row_max)
        alpha = jnp.exp2(m_old - m_new)
        p_blocks = [jnp.exp2(blk - m_new) for blk in blocks]
        l = alpha * l + functools.reduce(jnp.add, p_blocks)
        p = jnp.concatenate(p_blocks, axis=1).astype(BF16)
        acc = alpha * acc + jnp.dot(p, v_ref[0, 0, rows(j), :], preferred_element_type=F32)
        return m_new, l, acc

    def scores(qi, j):
        q = jnp.concatenate([q_ref[0, 0, rows(qi), :], qx_ref[0, 0, rows(qi), :]], axis=1)
        kt = jnp.concatenate([k_ref[0, 0, rows(j), :], kx_ref[0, 0, rows(j), :]], axis=1)
        s = lax.dot_general(q, kt, nt_dims, preferred_element_type=F32)
        return jnp.where(causal, s, MASK_VALUE) if j == qi else s

    pairs = [(qi, j) for qi in range(n_tiles) for j in range(qi + 1)]
    pending = [scores(*pair) for pair in pairs[:SCORE_LOOKAHEAD]]
    carry = None
    for n, (qi, j) in enumerate(pairs):
        if n + SCORE_LOOKAHEAD < len(pairs):
            pending.append(scores(*pairs[n + SCORE_LOOKAHEAD]))
        if j == 0:
            carry = (jnp.full((tile, LANES), -jnp.inf, F32), jnp.zeros((tile, LANES), F32),
                     jnp.zeros((tile, HEAD_DIM), F32))
        carry = update(carry, pending.pop(0), j)
        if j == qi:
            _, l, acc = carry
            l = jnp.sum(l, axis=-1, keepdims=True)
            out = ((acc / l) * gate_ref[0, 0, rows(qi), :]).astype(o_ref.dtype)
            first = qi * tile
            skip = max(HEAD_ROWS - first, 0)
            if skip < tile:
                o_ref[0, pl.ds(first + skip - HEAD_ROWS, tile - skip), :] = out[skip:]


def _fox_attention(q, qx, k, kx, v, gate):
    bsz, n_heads, lp, _ = q.shape
    tile = _pick_tile(lp, (384, 128))
    spec = pl.BlockSpec((1, 1, lp, HEAD_DIM), lambda b, h: (b, h, 0, 0))
    return pl.pallas_call(
        functools.partial(_fox_attention_kernel, tile=tile),
        grid=(bsz, n_heads),
        in_specs=[spec] * 6,
        out_specs=pl.BlockSpec((1, lp - HEAD_ROWS, HEAD_DIM), lambda b, h: (b, 0, h)),
        out_shape=jax.ShapeDtypeStruct((bsz, lp - HEAD_ROWS, n_heads * HEAD_DIM), BF16),
        compiler_params=_params(("parallel", "parallel")),
        name="fox_attention",
    )(q, qx, k, kx, v, gate)


def _out_proj_final_kernel(a_ref, w_ref, r_ref, o_ref):
    o_ref[0] = r_ref[0] + jnp.dot(a_ref[0], w_ref[...], preferred_element_type=F32)


def _out_proj_final(a, w, h):
    bsz, seq, d = a.shape
    tm = _pick_tile(seq, (512, 256, 128))
    return pl.pallas_call(
        _out_proj_final_kernel,
        grid=(bsz, seq // tm),
        in_specs=[pl.BlockSpec((1, tm, d), lambda b, i: (b, i, 0)),
                  pl.BlockSpec((d, d), lambda b, i: (0, 0)),
                  pl.BlockSpec((pl.Element(1), pl.Element(tm), pl.Element(d)),
                               lambda b, i: (b, pl.multiple_of(HEAD_ROWS + i * tm, HEAD_ROWS), 0))],
        out_specs=pl.BlockSpec((1, tm, d), lambda b, i: (b, i, 0)),
        out_shape=jax.ShapeDtypeStruct((bsz, seq, d), F32),
        compiler_params=_params(("parallel", "parallel")),
        name="out_proj_final",
    )(a, w, h)


def kernel(x, meta, gamma_lb, a_norm, a_w_in, a_out_norm, a_w_out, kv_norm, kv_w, fox_b_f, fox_k_norm,
           b_norm, b_w_in, b_q_norm, b_w_out):
    bsz, seq, d = x.shape
    n_heads = d // HEAD_DIM
    lp = HEAD_ROWS + seq
    assert a_norm.shape[0] == 1 and b_norm.shape[0] == 1, "one HGRN2 layer followed by one attention layer"

    h0 = jnp.concatenate([
        jnp.zeros((bsz, PAD_ROWS, d), x.dtype),
        jnp.broadcast_to(meta[None].astype(x.dtype), (bsz, N_META, d)),
        x], axis=1)

    u = _norm_proj(h0, a_norm[0], a_w_in[0].astype(BF16))
    g = _hgrn2(u, gamma_lb, a_out_norm[0])

    w_f = jnp.zeros((d, LANES), BF16).at[:, :n_heads].set(kv_w[:, 2 * d:].astype(BF16))
    b_f = jnp.zeros((1, LANES), F32).at[0, :n_heads].set(fox_b_f.astype(F32))
    h1, x_kv, x_b, logf = _out_proj_norms(g.reshape(bsz * lp, d), a_w_out[0].astype(BF16),
                                          h0.reshape(bsz * lp, d), kv_norm, b_norm[0], w_f, b_f)
    x_kv = x_kv.reshape(bsz, lp, d)
    x_b = x_b.reshape(bsz, lp, d)

    k_s = _head_proj(x_kv, kv_w[:, :d].astype(BF16), mode="headnorm", out_dtype=BF16, head_gain=fox_k_norm)
    v_s = _head_proj(x_kv, kv_w[:, d:2 * d].astype(BF16), mode="raw", out_dtype=BF16)
    qx, kx = _forget_features(logf.reshape(bsz, lp, LANES), n_heads)
    q = _head_proj(x_b, b_w_in[0][:, :d].astype(BF16), mode="headnorm", out_dtype=BF16,
                   head_gain=b_q_norm[0], scale=HEAD_DIM ** -0.5 * LOG2E)
    gate = _head_proj(x_b, b_w_in[0][:, d:].astype(BF16), mode="silu", out_dtype=F32)
    o = _fox_attention(q, qx, k_s, kx, v_s, gate)
    return _out_proj_final(o, b_w_out[0].astype(BF16), h1.reshape(bsz, lp, d))
```

```python
import functools

import jax
import jax.numpy as jnp
from jax import lax
from jax.experimental import pallas as pl
from jax.experimental.pallas import tpu as pltpu

F32 = jnp.float32
BF16 = jnp.bfloat16

HEAD_DIM = 128
N_META = 16
Q_BLOCK = 128
PAD_ROWS = Q_BLOCK - N_META
HEAD_ROWS = PAD_ROWS + N_META
CHUNK = 64
EPS = 1e-6
MASK_VALUE = -1e30
LANES = 128
LOG2E = 1.4426950408889634
VMEM_LIMIT_BYTES = 60 * 1024 * 1024


def _pick_tile(n, candidates):
    for c in candidates:
        if n % c == 0:
            return c
    raise ValueError(f"no tile in {candidates} divides {n}")


def _params(semantics):
    return pltpu.CompilerParams(dimension_semantics=semantics, vmem_limit_bytes=VMEM_LIMIT_BYTES)


def _sigmoid(x):
    return 1.0 / (1.0 + jnp.exp(-x))


def _rms(x):
    return x * lax.rsqrt(jnp.mean(x * x, axis=-1, keepdims=True) + EPS)


HEADS_PER_STEP = 2
SECTIONS = 4
PROJ_CHUNKS_AT_STAGE = (2, 2, 1, 1, 2)
PROJ_K_CHUNKS = sum(PROJ_CHUNKS_AT_STAGE)


def _hgrn2_head(u_ref, lb, g_out, st, first_row, side_work):
    rows = u_ref.shape[0]
    n_chunks = rows // CHUNK
    t_idx = lax.broadcasted_iota(jnp.int32, (CHUNK, CHUNK), 0)
    s_idx = lax.broadcasted_iota(jnp.int32, (CHUNK, CHUNK), 1)
    causal = s_idx <= t_idx
    tri = causal.astype(BF16)
    nt_dims = (((1,), (1,)), ((), ()))
    tn_dims = (((0,), (0,)), ((), ()))

    def chunk(a, c):
        return a[c * CHUNK:(c + 1) * CHUNK]

    side_work[0]()
    q_raw = u_ref[:, 0 * HEAD_DIM:1 * HEAD_DIM]
    f_raw = u_ref[:, 1 * HEAD_DIM:2 * HEAD_DIM]
    valid = (first_row + lax.broadcasted_iota(jnp.int32, (rows, 1), 0)) >= PAD_ROWS
    q = q_raw * _sigmoid(q_raw)
    fg = lb + (1.0 - lb) * _sigmoid(f_raw)
    logf = jnp.where(valid, jnp.log(fg), 0.0)
    k = jnp.where(valid, 1.0 - fg, 0.0)
    v_b = u_ref[:, 2 * HEAD_DIM:3 * HEAD_DIM].astype(BF16)

    pieces, rest = [], logf
    for _ in range(3):
        piece = rest.astype(BF16)
        pieces.append(piece)
        rest = rest - piece.astype(F32)
    stacked = jnp.concatenate(pieces, axis=1)
    b = []
    for c in range(n_chunks):
        parts = jnp.dot(tri, chunk(stacked, c), preferred_element_type=F32)
        b.append(parts[:, :HEAD_DIM] + parts[:, HEAD_DIM:2 * HEAD_DIM] + parts[:, 2 * HEAD_DIM:])
    side_work[1]()

    q_intra, k_intra, q_inter, k_dec, decay = [], [], [], [], []
    for c in range(n_chunks):
        b_mid = b[c][CHUNK // 2 - 1:CHUNK // 2, :]
        b_last = b[c][CHUNK - 1:CHUNK, :]
        q_c, k_c = chunk(q, c), chunk(k, c)
        q_intra.append((q_c * jnp.exp(b[c] - b_mid)).astype(BF16))
        k_intra.append((k_c * jnp.exp(b_mid - b[c])).astype(BF16))
        q_inter.append((q_c * jnp.exp(b[c])).astype(BF16))
        k_dec.append((k_c * jnp.exp(b_last - b[c])).astype(BF16))
        decay.append(jnp.exp(b_last))
    a = [lax.dot_general(q_intra[c], k_intra[c], nt_dims, preferred_element_type=F32) for c in range(n_chunks)]
    side_work[2]()

    a = [jnp.where(causal, a_c, 0.0).astype(BF16) for a_c in a]
    o_intra = [jnp.dot(a[c], chunk(v_b, c), preferred_element_type=F32) for c in range(n_chunks)]
    d_st = [lax.dot_general(chunk(v_b, c), k_dec[c], tn_dims, preferred_element_type=F32)
            for c in range(n_chunks)]
    side_work[3]()

    st_before = []
    for c in range(n_chunks):
        st_before.append(st.astype(BF16))
        st = decay[c] * st + d_st[c]
    o = jnp.concatenate(
        [o_intra[c] + lax.dot_general(q_inter[c], st_before[c], nt_dims, preferred_element_type=F32)
         for c in range(n_chunks)], axis=0)
    side_work[4]()

    z = u_ref[:, 3 * HEAD_DIM:4 * HEAD_DIM]
    return _rms(o) * g_out * (z * _sigmoid(z)), st


def _hgrn2_layer_kernel(x_ref, g_ref, w_first_ref, w_odd_ref, w_even_ref, gam_ref, go_ref, o_ref,
                        xn_ref, u_even_ref, u_odd_ref, st_ref, *, rows_per_step):
    r = pl.program_id(1)
    k = pl.program_id(2)
    last_k = pl.num_programs(2) - 1
    last_r = pl.num_programs(1) - 1
    d = xn_ref.shape[2]
    k_chunk = d // PROJ_K_CHUNKS
    this_tile = r % 2
    next_tile = (r + 1) % 2
    prefetch_next_tile = (k == last_k) & (r < last_r)

    def normalise(slot):
        xn_ref[slot] = (_rms(x_ref[0]) * g_ref[...]).astype(BF16)

    @pl.when((r == 0) & (k == 0))
    def _():
        st_ref[...] = jnp.zeros_like(st_ref)
        normalise(0)
        u_even_ref[...] = jnp.dot(xn_ref[0], w_first_ref[0], preferred_element_type=F32)

    @pl.when(prefetch_next_tile)
    def _():
        normalise(next_tile)

    gam = gam_ref[...]
    e = jnp.exp(gam - jnp.max(gam, axis=0, keepdims=True))
    lb = e[0:1] / jnp.sum(e, axis=0, keepdims=True)
    g_out = go_ref[...]
    first_row = r * rows_per_step

    def head(local, u_ref, u_next_ref, w_next_ref, xn_slot):
        acc = []

        def project_chunks(first, count):
            def work():
                for c in range(first, first + count):
                    cols = pl.ds(c * k_chunk, k_chunk)
                    part = jnp.dot(xn_ref[xn_slot, :, cols], w_next_ref[0, cols, :], preferred_element_type=F32)
                    acc[:] = [part if not acc else acc[0] + part]
            return work

        starts = [sum(PROJ_CHUNKS_AT_STAGE[:n]) for n in range(len(PROJ_CHUNKS_AT_STAGE))]
        lanes = slice(local * HEAD_DIM, (local + 1) * HEAD_DIM)
        idx = HEADS_PER_STEP * k + local
        out, st = _hgrn2_head(u_ref, lb[:, lanes], g_out[:, lanes], st_ref[idx], first_row,
                              [project_chunks(s, n) for s, n in zip(starts, PROJ_CHUNKS_AT_STAGE)])
        u_next_ref[...] = acc[0]
        st_ref[idx] = st
        o_ref[0, :, lanes] = out.astype(o_ref.dtype)

    head(0, u_even_ref, u_odd_ref, w_odd_ref, this_tile)
    head(1, u_odd_ref, u_even_ref, w_even_ref, jnp.where(prefetch_next_tile, next_tile, this_tile))


def _hgrn2_layer(h, g_norm, w_in, gamma_lb, g_out):
    bsz, lp, d = h.shape
    n_heads = d // HEAD_DIM
    n_pairs = n_heads // HEADS_PER_STEP
    assert n_heads % HEADS_PER_STEP == 0 and n_pairs >= 2 and d % PROJ_K_CHUNKS == 0
    rows = _pick_tile(lp, (704, 384, 128, 64))
    n_row_tiles = lp // rows
    n_lb = gamma_lb.shape[0]
    width = HEADS_PER_STEP * HEAD_DIM
    w_heads = (w_in.reshape(d, SECTIONS, n_heads, HEAD_DIM).transpose(2, 0, 1, 3)
               .reshape(n_heads, d, SECTIONS * HEAD_DIM).astype(BF16))
    w_block = (1, d, SECTIONS * HEAD_DIM)

    def x_tile(b, r, k):
        return b, jnp.where(k == n_pairs - 1, jnp.minimum(r + 1, n_row_tiles - 1), r), 0

    return pl.pallas_call(
        functools.partial(_hgrn2_layer_kernel, rows_per_step=rows),
        grid=(bsz, n_row_tiles, n_pairs),
        in_specs=[pl.BlockSpec((1, rows, d), x_tile),
                  pl.BlockSpec((1, d), lambda b, r, k: (0, 0)),
                  pl.BlockSpec(w_block, lambda b, r, k: (0, 0, 0)),
                  pl.BlockSpec(w_block, lambda b, r, k: (HEADS_PER_STEP * k + 1, 0, 0)),
                  pl.BlockSpec(w_block, lambda b, r, k: ((HEADS_PER_STEP * k + 2) % n_heads, 0, 0)),
                  pl.BlockSpec((n_lb, width), lambda b, r, k: (0, k)),
                  pl.BlockSpec((1, width), lambda b, r, k: (0, k))],
        out_specs=pl.BlockSpec((1, rows, width), lambda b, r, k: (b, r, k)),
        out_shape=jax.ShapeDtypeStruct((bsz, lp, d), BF16),
        scratch_shapes=[pltpu.VMEM((2, rows, d), BF16),
                        pltpu.VMEM((rows, SECTIONS * HEAD_DIM), F32),
                        pltpu.VMEM((rows, SECTIONS * HEAD_DIM), F32),
                        pltpu.VMEM((n_heads, HEAD_DIM, HEAD_DIM), F32)],
        compiler_params=_params(("parallel", "arbitrary", "arbitrary")),
        name="hgrn2_layer",
    )(h, g_norm.reshape(1, d).astype(F32), w_heads, w_heads, w_heads,
      gamma_lb.astype(F32), g_out.reshape(1, d).astype(F32))


def _out_proj_norms_kernel(a_ref, w_ref, r_ref, gkv_ref, gb_ref, wf_ref, bf_ref,
                           h_ref, xkv_ref, xb_ref, lf_ref):
    h = r_ref[...] + jnp.dot(a_ref[...], w_ref[...], preferred_element_type=F32)
    h_ref[...] = h
    y = _rms(h)
    x_kv = (y * gkv_ref[...]).astype(BF16)
    xkv_ref[...] = x_kv
    xb_ref[...] = (y * gb_ref[...]).astype(BF16)
    fl = jnp.dot(x_kv, wf_ref[...], preferred_element_type=F32) + bf_ref[...]
    lf_ref[...] = jnp.minimum(fl, 0.0) - jnp.log(1.0 + jnp.exp(-jnp.abs(fl)))


def _out_proj_norms(a, w, res, g_kv, g_b, w_f, b_f):
    m, d = a.shape
    tm = _pick_tile(m, (528, 384, 256, 128))
    row_spec = pl.BlockSpec((tm, d), lambda i: (i, 0))
    vec_spec = pl.BlockSpec((1, d), lambda i: (0, 0))
    return pl.pallas_call(
        _out_proj_norms_kernel,
        grid=(m // tm,),
        in_specs=[row_spec, pl.BlockSpec((d, d), lambda i: (0, 0)), row_spec, vec_spec, vec_spec,
                  pl.BlockSpec((d, LANES), lambda i: (0, 0)), pl.BlockSpec((1, LANES), lambda i: (0, 0))],
        out_specs=[row_spec, row_spec, row_spec, pl.BlockSpec((tm, LANES), lambda i: (i, 0))],
        out_shape=[jax.ShapeDtypeStruct((m, d), F32), jax.ShapeDtypeStruct((m, d), BF16),
                   jax.ShapeDtypeStruct((m, d), BF16), jax.ShapeDtypeStruct((m, LANES), F32)],
        compiler_params=_params(("parallel",)),
        name="out_proj_norms",
    )(a, w, res, g_kv.reshape(1, d).astype(F32), g_b.reshape(1, d).astype(F32), w_f, b_f)


PROJ_COLS_PER_DOT = 512


def _head_proj_kernel(*refs, mode, scale):
    if mode == "headnorm":
        x_ref, w_ref, hg_ref, o_ref = refs
    else:
        x_ref, w_ref, o_ref = refs
    x = x_ref[0]
    n_out = w_ref.shape[1]
    cols_per_dot = min(PROJ_COLS_PER_DOT, n_out)
    assert n_out % cols_per_dot == 0
    heads_per_dot = cols_per_dot // HEAD_DIM
    for c in range(n_out // cols_per_dot):
        cols = pl.ds(c * cols_per_dot, cols_per_dot)
        acc = jnp.dot(x, w_ref[:, cols], preferred_element_type=F32)
        for n in range(heads_per_dot):
            a = acc[:, n * HEAD_DIM:(n + 1) * HEAD_DIM]
            head = c * heads_per_dot + n
            if mode == "headnorm":
                a = _rms(a) * hg_ref[:, head * HEAD_DIM:(head + 1) * HEAD_DIM]
                if scale is not None:
                    a = a * scale
            elif mode == "silu":
                a = a * _sigmoid(a)
            o_ref[0, head] = a.astype(o_ref.dtype)


def _head_proj(x, w, section, *, mode, out_dtype, head_gain=None, scale=None):
    bsz, lp, d = x.shape
    n_out = d
    tm = _pick_tile(lp, (1056, 704, 384, 128))
    in_specs = [pl.BlockSpec((1, tm, d), lambda b, i: (b, i, 0)),
                pl.BlockSpec((d, n_out), lambda b, i: (0, section))]
    args = [x, w]
    if mode == "headnorm":
        in_specs.append(pl.BlockSpec((1, n_out), lambda b, i: (0, 0)))
        args.append(head_gain.reshape(1, n_out).astype(F32))
    return pl.pallas_call(
        functools.partial(_head_proj_kernel, mode=mode, scale=scale),
        grid=(bsz, lp // tm),
        in_specs=in_specs,
        out_specs=pl.BlockSpec((1, n_out // HEAD_DIM, tm, HEAD_DIM), lambda b, i: (b, 0, i, 0)),
        out_shape=jax.ShapeDtypeStruct((bsz, n_out // HEAD_DIM, lp, HEAD_DIM), out_dtype),
        compiler_params=_params(("parallel", "parallel")),
        name=f"head_proj_{mode}",
    )(*args)


N_F_PIECES = 3


def _feature_scatter(n_heads):
    r = jnp.arange((N_F_PIECES + 1) * LANES)[:, None]
    c = jnp.arange(n_heads * LANES)[None, :]
    piece, src = r // LANES, r % LANES
    head, lane = c // LANES, c % LANES
    is_piece = (piece < N_F_PIECES) & (src == head)
    is_one = (piece == N_F_PIECES) & (src == 0)
    sel_q = (is_piece & (lane == piece)) | (is_one & (lane >= N_F_PIECES) & (lane < 2 * N_F_PIECES))
    sel_k = (is_one & (lane < N_F_PIECES)).astype(F32) - (is_piece & (lane == N_F_PIECES + piece)).astype(F32)
    return sel_q.astype(BF16), sel_k.astype(BF16)


def _forget_features_kernel(lf_ref, selq_ref, selk_ref, qx_ref, kx_ref, carry_ref, *, n_heads, block_rows):
    i = pl.program_id(1)

    @pl.when(i == 0)
    def _():
        carry_ref[...] = jnp.zeros_like(carry_ref)

    t_idx = lax.broadcasted_iota(jnp.int32, (Q_BLOCK, Q_BLOCK), 0)
    s_idx = lax.broadcasted_iota(jnp.int32, (Q_BLOCK, Q_BLOCK), 1)
    tri = (s_idx <= t_idx).astype(F32)
    lane = lax.broadcasted_iota(jnp.int32, (Q_BLOCK, LANES), 1)
    pad_key = jnp.where(lane == N_F_PIECES, MASK_VALUE, 0.0)
    ones = jnp.ones((Q_BLOCK, LANES), BF16)
    carry = carry_ref[...]
    for sub in range(block_rows // Q_BLOCK):
        rows = pl.ds(sub * Q_BLOCK, Q_BLOCK)
        row = i * block_rows + sub * Q_BLOCK + lax.broadcasted_iota(jnp.int32, (Q_BLOCK, 1), 0)
        real = row >= PAD_ROWS
        x = jnp.where(real, lf_ref[0, rows, :], 0.0)
        c = jnp.dot(tri, x, precision=lax.Precision.HIGHEST, preferred_element_type=F32) + carry
        carry = c[Q_BLOCK - 1:Q_BLOCK, :]
        f = c * LOG2E
        pieces = []
        for _ in range(N_F_PIECES):
            piece = f.astype(BF16)
            pieces.append(piece)
            f = f - piece.astype(F32)
        stacked = jnp.concatenate(pieces + [ones], axis=1)
        qx = jnp.dot(stacked, selq_ref[...], preferred_element_type=F32)
        kx = jnp.dot(stacked, selk_ref[...], preferred_element_type=F32)
        for h in range(n_heads):
            head = slice(h * LANES, (h + 1) * LANES)
            qx_ref[0, h, rows, :] = qx[:, head].astype(BF16)
            kx_ref[0, h, rows, :] = jnp.where(real, kx[:, head], pad_key).astype(BF16)
    carry_ref[...] = carry


def _forget_features(logf, n_heads):
    bsz, lp, _ = logf.shape
    block_rows = _pick_tile(lp, (384, 128))
    sel_q, sel_k = _feature_scatter(n_heads)
    out = jax.ShapeDtypeStruct((bsz, n_heads, lp, LANES), BF16)
    spec = pl.BlockSpec((1, n_heads, block_rows, LANES), lambda b, i: (b, 0, i, 0))
    sel_spec = pl.BlockSpec(sel_q.shape, lambda b, i: (0, 0))
    return pl.pallas_call(
        functools.partial(_forget_features_kernel, n_heads=n_heads, block_rows=block_rows),
        grid=(bsz, lp // block_rows),
        in_specs=[pl.BlockSpec((1, block_rows, LANES), lambda b, i: (b, i, 0)), sel_spec, sel_spec],
        out_specs=[spec, spec],
        out_shape=[out, out],
        scratch_shapes=[pltpu.VMEM((1, LANES), F32)],
        compiler_params=_params(("parallel", "arbitrary")),
        name="forget_features",
    )(logf, sel_q, sel_k)


SCORE_LOOKAHEAD = 2


def _fox_attention_kernel(q_ref, qx_ref, k_ref, kx_ref, v_ref, gate_ref, o_ref, *, tile):
    nt_dims = (((1,), (1,)), ((), ()))
    n_tiles = q_ref.shape[2] // tile
    row = lax.broadcasted_iota(jnp.int32, (tile, tile), 0)
    col = lax.broadcasted_iota(jnp.int32, (tile, tile), 1)
    causal = col <= row

    def rows(t):
        return pl.ds(t * tile, tile)

    def update(carry, s, j):
        m_old, l, acc = carry
        blocks = [s[:, c * LANES:(c + 1) * LANES] for c in range(tile // LANES)]
        row_max = jnp.max(functools.reduce(jnp.maximum, blocks), axis=-1, keepdims=True)
        m_new = jnp.maximum(m_old, row_max)
        alpha = jnp.exp2(m_old - m_new)
        p_blocks = [jnp.exp2(blk - m_new) for blk in blocks]
        l = alpha * l + functools.reduce(jnp.add, p_blocks)
        p = jnp.concatenate(p_blocks, axis=1).astype(BF16)
        acc = alpha * acc + jnp.dot(p, v_ref[0, 0, rows(j), :], preferred_element_type=F32)
        return m_new, l, acc

    def scores(qi, j):
        q = jnp.concatenate([q_ref[0, 0, rows(qi), :], qx_ref[0, 0, rows(qi), :]], axis=1)
        kt = jnp.concatenate([k_ref[0, 0, rows(j), :], kx_ref[0, 0, rows(j), :]], axis=1)
        s = lax.dot_general(q, kt, nt_dims, preferred_element_type=F32)
        return jnp.where(causal, s, MASK_VALUE) if j == qi else s

    pairs = [(qi, j) for qi in range(n_tiles) for j in range(qi + 1)]
    pending = [scores(*pair) for pair in pairs[:SCORE_LOOKAHEAD]]
    carry = None
    for n, (qi, j) in enumerate(pairs):
        if n + SCORE_LOOKAHEAD < len(pairs):
            pending.append(scores(*pairs[n + SCORE_LOOKAHEAD]))
        if j == 0:
            carry = (jnp.full((tile, LANES), -jnp.inf, F32), jnp.zeros((tile, LANES), F32),
                     jnp.zeros((tile, HEAD_DIM), F32))
        carry = update(carry, pending.pop(0), j)
        if j == qi:
            _, l, acc = carry
            l = jnp.sum(l, axis=-1, keepdims=True)
            out = ((acc / l) * gate_ref[0, 0, rows(qi), :]).astype(o_ref.dtype)
            first = qi * tile
            skip = max(HEAD_ROWS - first, 0)
            if skip < tile:
                o_ref[0, pl.ds(first + skip - HEAD_ROWS, tile - skip), :] = out[skip:]


def _fox_attention(q, qx, k, kx, v, gate):
    bsz, n_heads, lp, _ = q.shape
    tile = _pick_tile(lp, (384, 128))
    spec = pl.BlockSpec((1, 1, lp, HEAD_DIM), lambda b, h: (b, h, 0, 0))
    return pl.pallas_call(
        functools.partial(_fox_attention_kernel, tile=tile),
        grid=(bsz, n_heads),
        in_specs=[spec] * 6,
        out_specs=pl.BlockSpec((1, lp - HEAD_ROWS, HEAD_DIM), lambda b, h: (b, 0, h)),
        out_shape=jax.ShapeDtypeStruct((bsz, lp - HEAD_ROWS, n_heads * HEAD_DIM), BF16),
        compiler_params=_params(("parallel", "parallel")),
        name="fox_attention",
    )(q, qx, k, kx, v, gate)


def _out_proj_final_kernel(a_ref, w_ref, r_ref, o_ref):
    o_ref[0] = r_ref[0] + jnp.dot(a_ref[0], w_ref[...], preferred_element_type=F32)


def _out_proj_final(a, w, h):
    bsz, seq, d = a.shape
    tm = _pick_tile(seq, (512, 256, 128))
    return pl.pallas_call(
        _out_proj_final_kernel,
        grid=(bsz, seq // tm),
        in_specs=[pl.BlockSpec((1, tm, d), lambda b, i: (b, i, 0)),
                  pl.BlockSpec((d, d), lambda b, i: (0, 0)),
                  pl.BlockSpec((pl.Element(1), pl.Element(tm), pl.Element(d)),
                               lambda b, i: (b, pl.multiple_of(HEAD_ROWS + i * tm, HEAD_ROWS), 0))],
        out_specs=pl.BlockSpec((1, tm, d), lambda b, i: (b, i, 0)),
        out_shape=jax.ShapeDtypeStruct((bsz, seq, d), F32),
        compiler_params=_params(("parallel", "parallel")),
        name="out_proj_final",
    )(a, w, h)


def kernel(x, meta, gamma_lb, a_norm, a_w_in, a_out_norm, a_w_out, kv_norm, kv_w, fox_b_f, fox_k_norm,
           b_norm, b_w_in, b_q_norm, b_w_out):
    bsz, seq, d = x.shape
    n_heads = d // HEAD_DIM
    lp = HEAD_ROWS + seq
    assert a_norm.shape[0] == 1 and b_norm.shape[0] == 1, "one HGRN2 layer followed by one attention layer"

    h0 = jnp.concatenate([
        jnp.zeros((bsz, PAD_ROWS, d), x.dtype),
        jnp.broadcast_to(meta[None].astype(x.dtype), (bsz, N_META, d)),
        x], axis=1)

    g = _hgrn2_layer(h0, a_norm[0], a_w_in[0], gamma_lb, a_out_norm[0])

    w_f = jnp.zeros((d, LANES), BF16).at[:, :n_heads].set(kv_w[:, 2 * d:].astype(BF16))
    b_f = jnp.zeros((1, LANES), F32).at[0, :n_heads].set(fox_b_f.astype(F32))
    h1, x_kv, x_b, logf = _out_proj_norms(g.reshape(bsz * lp, d), a_w_out[0].astype(BF16),
                                          h0.reshape(bsz * lp, d), kv_norm, b_norm[0], w_f, b_f)
    x_kv = x_kv.reshape(bsz, lp, d)
    x_b = x_b.reshape(bsz, lp, d)

    kv_w16 = kv_w.astype(BF16)
    k_s = _head_proj(x_kv, kv_w16, 0, mode="headnorm", out_dtype=BF16, head_gain=fox_k_norm)
    v_s = _head_proj(x_kv, kv_w16, 1, mode="raw", out_dtype=BF16)
    qx, kx = _forget_features(logf.reshape(bsz, lp, LANES), n_heads)
    b_w16 = b_w_in[0].astype(BF16)
    q = _head_proj(x_b, b_w16, 0, mode="headnorm", out_dtype=BF16,
                   head_gain=b_q_norm[0], scale=HEAD_DIM ** -0.5 * LOG2E)
    gate = _head_proj(x_b, b_w16, 1, mode="silu", out_dtype=F32)
    o = _fox_attention(q, qx, k_s, kx, v_s, gate)
    return _out_proj_final(o, b_w_out[0].astype(BF16), h1.reshape(bsz, lp, d))
```

```python
import functools

import jax
import jax.numpy as jnp
from jax import lax
from jax.experimental import pallas as pl
from jax.experimental.pallas import tpu as pltpu

F32 = jnp.float32
BF16 = jnp.bfloat16

HEAD_DIM = 128
N_META = 16
Q_BLOCK = 128
PAD_ROWS = Q_BLOCK - N_META
HEAD_ROWS = PAD_ROWS + N_META
CHUNK = 64
EPS = 1e-6
MASK_VALUE = -1e30
LANES = 128
LOG2E = 1.4426950408889634
VMEM_LIMIT_BYTES = 60 * 1024 * 1024


def _pick_tile(n, candidates):
    for c in candidates:
        if n % c == 0:
            return c
    raise ValueError(f"no tile in {candidates} divides {n}")


def _params(semantics):
    return pltpu.CompilerParams(dimension_semantics=semantics, vmem_limit_bytes=VMEM_LIMIT_BYTES)


def _sigmoid(x):
    return 1.0 / (1.0 + jnp.exp(-x))


def _rms(x):
    return x * lax.rsqrt(jnp.mean(x * x, axis=-1, keepdims=True) + EPS)


HEADS_PER_STEP = 2
SECTIONS = 4
PROJ_CHUNKS_AT_STAGE = (2, 2, 1, 1, 2)
PROJ_K_CHUNKS = sum(PROJ_CHUNKS_AT_STAGE)


def _hgrn2_head(u_ref, lb, g_out, st, first_row, side_work):
    rows = u_ref.shape[0]
    n_chunks = rows // CHUNK
    t_idx = lax.broadcasted_iota(jnp.int32, (CHUNK, CHUNK), 0)
    s_idx = lax.broadcasted_iota(jnp.int32, (CHUNK, CHUNK), 1)
    causal = s_idx <= t_idx
    tri = causal.astype(BF16)
    nt_dims = (((1,), (1,)), ((), ()))
    tn_dims = (((0,), (0,)), ((), ()))

    def chunk(a, c):
        return a[c * CHUNK:(c + 1) * CHUNK]

    side_work[0]()
    q_raw = u_ref[:, 0 * HEAD_DIM:1 * HEAD_DIM]
    f_raw = u_ref[:, 1 * HEAD_DIM:2 * HEAD_DIM]
    valid = (first_row + lax.broadcasted_iota(jnp.int32, (rows, 1), 0)) >= PAD_ROWS
    q = q_raw * _sigmoid(q_raw)
    fg = lb + (1.0 - lb) * _sigmoid(f_raw)
    logf = jnp.where(valid, jnp.log(fg), 0.0)
    k = jnp.where(valid, 1.0 - fg, 0.0)
    v_b = u_ref[:, 2 * HEAD_DIM:3 * HEAD_DIM].astype(BF16)

    pieces, rest = [], logf
    for _ in range(3):
        piece = rest.astype(BF16)
        pieces.append(piece)
        rest = rest - piece.astype(F32)
    stacked = jnp.concatenate(pieces, axis=1)
    b = []
    for c in range(n_chunks):
        parts = jnp.dot(tri, chunk(stacked, c), preferred_element_type=F32)
        b.append(parts[:, :HEAD_DIM] + parts[:, HEAD_DIM:2 * HEAD_DIM] + parts[:, 2 * HEAD_DIM:])
    side_work[1]()

    q_intra, k_intra, q_inter, k_dec, decay = [], [], [], [], []
    for c in range(n_chunks):
        b_mid = b[c][CHUNK // 2 - 1:CHUNK // 2, :]
        b_last = b[c][CHUNK - 1:CHUNK, :]
        q_c, k_c = chunk(q, c), chunk(k, c)
        q_intra.append((q_c * jnp.exp(b[c] - b_mid)).astype(BF16))
        k_intra.append((k_c * jnp.exp(b_mid - b[c])).astype(BF16))
        q_inter.append((q_c * jnp.exp(b[c])).astype(BF16))
        k_dec.append((k_c * jnp.exp(b_last - b[c])).astype(BF16))
        decay.append(jnp.exp(b_last))
    a = [lax.dot_general(q_intra[c], k_intra[c], nt_dims, preferred_element_type=F32) for c in range(n_chunks)]
    side_work[2]()

    a = [jnp.where(causal, a_c, 0.0).astype(BF16) for a_c in a]
    o_intra = [jnp.dot(a[c], chunk(v_b, c), preferred_element_type=F32) for c in range(n_chunks)]
    d_st = [lax.dot_general(chunk(v_b, c), k_dec[c], tn_dims, preferred_element_type=F32)
            for c in range(n_chunks)]
    side_work[3]()

    st_before = []
    for c in range(n_chunks):
        st_before.append(st.astype(BF16))
        st = decay[c] * st + d_st[c]
    o = jnp.concatenate(
        [o_intra[c] + lax.dot_general(q_inter[c], st_before[c], nt_dims, preferred_element_type=F32)
         for c in range(n_chunks)], axis=0)
    side_work[4]()

    z = u_ref[:, 3 * HEAD_DIM:4 * HEAD_DIM]
    return _rms(o) * g_out * (z * _sigmoid(z)), st


def _hgrn2_layer_kernel(x_ref, g_ref, w_first_ref, w_odd_ref, w_even_ref, gam_ref, go_ref, o_ref,
                        xn_ref, u_even_ref, u_odd_ref, st_ref, *, rows_per_step):
    r = pl.program_id(1)
    k = pl.program_id(2)
    last_k = pl.num_programs(2) - 1
    last_r = pl.num_programs(1) - 1
    d = xn_ref.shape[2]
    k_chunk = d // PROJ_K_CHUNKS
    this_tile = r % 2
    next_tile = (r + 1) % 2
    prefetch_next_tile = (k == last_k) & (r < last_r)

    def normalise(slot):
        xn_ref[slot] = (_rms(x_ref[0]) * g_ref[...]).astype(BF16)

    @pl.when((r == 0) & (k == 0))
    def _():
        st_ref[...] = jnp.zeros_like(st_ref)
        normalise(0)
        u_even_ref[...] = jnp.dot(xn_ref[0], w_first_ref[0], preferred_element_type=F32)

    @pl.when(prefetch_next_tile)
    def _():
        normalise(next_tile)

    gam = gam_ref[...]
    e = jnp.exp(gam - jnp.max(gam, axis=0, keepdims=True))
    lb = e[0:1] / jnp.sum(e, axis=0, keepdims=True)
    g_out = go_ref[...]
    first_row = r * rows_per_step

    def head(local, u_ref, u_next_ref, w_next_ref, xn_slot):
        acc = []

        def project_chunks(first, count):
            def work():
                for c in range(first, first + count):
                    cols = pl.ds(c * k_chunk, k_chunk)
                    part = jnp.dot(xn_ref[xn_slot, :, cols], w_next_ref[0, cols, :], preferred_element_type=F32)
                    acc[:] = [part if not acc else acc[0] + part]
            return work

        starts = [sum(PROJ_CHUNKS_AT_STAGE[:n]) for n in range(len(PROJ_CHUNKS_AT_STAGE))]
        lanes = slice(local * HEAD_DIM, (local + 1) * HEAD_DIM)
        idx = HEADS_PER_STEP * k + local
        out, st = _hgrn2_head(u_ref, lb[:, lanes], g_out[:, lanes], st_ref[idx], first_row,
                              [project_chunks(s, n) for s, n in zip(starts, PROJ_CHUNKS_AT_STAGE)])
        u_next_ref[...] = acc[0]
        st_ref[idx] = st
        o_ref[0, :, lanes] = out.astype(o_ref.dtype)

    head(0, u_even_ref, u_odd_ref, w_odd_ref, this_tile)
    head(1, u_odd_ref, u_even_ref, w_even_ref, jnp.where(prefetch_next_tile, next_tile, this_tile))


def _head_major_weights_kernel(*refs):
    *section_refs, o_ref = refs
    o_ref[0] = jnp.concatenate([ref[...] for ref in section_refs], axis=1).astype(o_ref.dtype)


def _head_major_weights(w_in, n_heads):
    d = w_in.shape[0]

    def section(s):
        return pl.BlockSpec((d, HEAD_DIM), lambda n, s=s: (0, s * n_heads + n))

    return pl.pallas_call(
        _head_major_weights_kernel,
        grid=(n_heads,),
        in_specs=[section(s) for s in range(SECTIONS)],
        out_specs=pl.BlockSpec((1, d, SECTIONS * HEAD_DIM), lambda n: (n, 0, 0)),
        out_shape=jax.ShapeDtypeStruct((n_heads, d, SECTIONS * HEAD_DIM), BF16),
        compiler_params=_params(("parallel",)),
        name="head_major_weights",
    )(*([w_in] * SECTIONS))


def _hgrn2_layer(h, g_norm, w_in, gamma_lb, g_out):
    bsz, lp, d = h.shape
    n_heads = d // HEAD_DIM
    n_pairs = n_heads // HEADS_PER_STEP
    assert n_heads % HEADS_PER_STEP == 0 and n_pairs >= 2 and d % PROJ_K_CHUNKS == 0
    rows = _pick_tile(lp, (704, 384, 128, 64))
    n_row_tiles = lp // rows
    n_lb = gamma_lb.shape[0]
    width = HEADS_PER_STEP * HEAD_DIM
    w_heads = _head_major_weights(w_in, n_heads)
    w_block = (1, d, SECTIONS * HEAD_DIM)

    def x_tile(b, r, k):
        return b, jnp.where(k == n_pairs - 1, jnp.minimum(r + 1, n_row_tiles - 1), r), 0

    return pl.pallas_call(
        functools.partial(_hgrn2_layer_kernel, rows_per_step=rows),
        grid=(bsz, n_row_tiles, n_pairs),
        in_specs=[pl.BlockSpec((1, rows, d), x_tile),
                  pl.BlockSpec((1, d), lambda b, r, k: (0, 0)),
                  pl.BlockSpec(w_block, lambda b, r, k: (0, 0, 0)),
                  pl.BlockSpec(w_block, lambda b, r, k: (HEADS_PER_STEP * k + 1, 0, 0)),
                  pl.BlockSpec(w_block, lambda b, r, k: ((HEADS_PER_STEP * k + 2) % n_heads, 0, 0)),
                  pl.BlockSpec((n_lb, width), lambda b, r, k: (0, k)),
                  pl.BlockSpec((1, width), lambda b, r, k: (0, k))],
        out_specs=pl.BlockSpec((1, rows, width), lambda b, r, k: (b, r, k)),
        out_shape=jax.ShapeDtypeStruct((bsz, lp, d), BF16),
        scratch_shapes=[pltpu.VMEM((2, rows, d), BF16),
                        pltpu.VMEM((rows, SECTIONS * HEAD_DIM), F32),
                        pltpu.VMEM((rows, SECTIONS * HEAD_DIM), F32),
                        pltpu.VMEM((n_heads, HEAD_DIM, HEAD_DIM), F32)],
        compiler_params=_params(("parallel", "arbitrary", "arbitrary")),
        name="hgrn2_layer",
    )(h, g_norm.reshape(1, d).astype(F32), w_heads, w_heads, w_heads,
      gamma_lb.astype(F32), g_out.reshape(1, d).astype(F32))


OUT_PROJ_SUB_ROWS = (176, 128)


def _out_proj_norms_kernel(a_ref, w_ref, r_ref, gkv_ref, gb_ref, wf_ref, bf_ref,
                           h_ref, xkv_ref, xb_ref, lf_ref):
    tm = a_ref.shape[0]
    sub = _pick_tile(tm, OUT_PROJ_SUB_ROWS)

    def product(c):
        return jnp.dot(a_ref[pl.ds(c * sub, sub), :], w_ref[...], preferred_element_type=F32)

    acc = product(0)
    for c in range(tm // sub):
        acc_next = product(c + 1) if c + 1 < tm // sub else None
        rows = pl.ds(c * sub, sub)
        h = r_ref[rows, :] + acc
        h_ref[rows, :] = h
        y = _rms(h)
        x_kv = (y * gkv_ref[...]).astype(BF16)
        xkv_ref[rows, :] = x_kv
        xb_ref[rows, :] = (y * gb_ref[...]).astype(BF16)
        fl = jnp.dot(x_kv, wf_ref[...], preferred_element_type=F32) + bf_ref[...]
        lf_ref[rows, :] = jnp.minimum(fl, 0.0) - jnp.log(1.0 + jnp.exp(-jnp.abs(fl)))
        acc = acc_next


def _out_proj_norms(a, w, res, g_kv, g_b, w_f, b_f):
    m, d = a.shape
    tm = _pick_tile(m, (528, 384, 256, 128))
    row_spec = pl.BlockSpec((tm, d), lambda i: (i, 0))
    vec_spec = pl.BlockSpec((1, d), lambda i: (0, 0))
    return pl.pallas_call(
        _out_proj_norms_kernel,
        grid=(m // tm,),
        in_specs=[row_spec, pl.BlockSpec((d, d), lambda i: (0, 0)), row_spec, vec_spec, vec_spec,
                  pl.BlockSpec((d, LANES), lambda i: (0, 0)), pl.BlockSpec((1, LANES), lambda i: (0, 0))],
        out_specs=[row_spec, row_spec, row_spec, pl.BlockSpec((tm, LANES), lambda i: (i, 0))],
        out_shape=[jax.ShapeDtypeStruct((m, d), F32), jax.ShapeDtypeStruct((m, d), BF16),
                   jax.ShapeDtypeStruct((m, d), BF16), jax.ShapeDtypeStruct((m, LANES), F32)],
        compiler_params=_params(("parallel",)),
        name="out_proj_norms",
    )(a, w, res, g_kv.reshape(1, d).astype(F32), g_b.reshape(1, d).astype(F32), w_f, b_f)


PROJ_COLS_PER_DOT = 512


def _head_proj_kernel(*refs, mode, scale):
    if mode == "headnorm":
        x_ref, w_ref, hg_ref, o_ref = refs
    else:
        x_ref, w_ref, o_ref = refs
    x = x_ref[0]
    n_out = w_ref.shape[1]
    cols_per_dot = min(PROJ_COLS_PER_DOT, n_out)
    assert n_out % cols_per_dot == 0
    heads_per_dot = cols_per_dot // HEAD_DIM
    for c in range(n_out // cols_per_dot):
        cols = pl.ds(c * cols_per_dot, cols_per_dot)
        acc = jnp.dot(x, w_ref[:, cols], preferred_element_type=F32)
        for n in range(heads_per_dot):
            a = acc[:, n * HEAD_DIM:(n + 1) * HEAD_DIM]
            head = c * heads_per_dot + n
            if mode == "headnorm":
                a = _rms(a) * hg_ref[:, head * HEAD_DIM:(head + 1) * HEAD_DIM]
                if scale is not None:
                    a = a * scale
            elif mode == "silu":
                a = a * _sigmoid(a)
            o_ref[0, head] = a.astype(o_ref.dtype)


def _head_proj(x, w, section, *, mode, out_dtype, head_gain=None, scale=None):
    bsz, lp, d = x.shape
    n_out = d
    tm = _pick_tile(lp, (1056, 704, 384, 128))
    in_specs = [pl.BlockSpec((1, tm, d), lambda b, i: (b, i, 0)),
                pl.BlockSpec((d, n_out), lambda b, i: (0, section))]
    args = [x, w]
    if mode == "headnorm":
        in_specs.append(pl.BlockSpec((1, n_out), lambda b, i: (0, 0)))
        args.append(head_gain.reshape(1, n_out).astype(F32))
    return pl.pallas_call(
        functools.partial(_head_proj_kernel, mode=mode, scale=scale),
        grid=(bsz, lp // tm),
        in_specs=in_specs,
        out_specs=pl.BlockSpec((1, n_out // HEAD_DIM, tm, HEAD_DIM), lambda b, i: (b, 0, i, 0)),
        out_shape=jax.ShapeDtypeStruct((bsz, n_out // HEAD_DIM, lp, HEAD_DIM), out_dtype),
        compiler_params=_params(("parallel", "parallel")),
        name=f"head_proj_{mode}",
    )(*args)


N_F_PIECES = 3


def _feature_scatter(n_heads):
    assert N_F_PIECES * n_heads < LANES
    r = jnp.arange(LANES)[:, None]
    c = jnp.arange(n_heads * LANES)[None, :]
    piece, src = r // n_heads, r % n_heads
    head, lane = c // LANES, c % LANES
    is_piece = (piece < N_F_PIECES) & (src == head)
    is_one = r == N_F_PIECES * n_heads
    sel_q = (is_piece & (lane == piece)) | (is_one & (lane >= N_F_PIECES) & (lane < 2 * N_F_PIECES))
    sel_k = (is_one & (lane < N_F_PIECES)).astype(F32) - (is_piece & (lane == N_F_PIECES + piece)).astype(F32)
    return sel_q.astype(BF16), sel_k.astype(BF16)


def _forget_features_kernel(lf_ref, selq_ref, selk_ref, qx_ref, kx_ref, carry_ref, *, n_heads, block_rows):
    i = pl.program_id(1)

    @pl.when(i == 0)
    def _():
        carry_ref[...] = jnp.zeros_like(carry_ref)

    t_idx = lax.broadcasted_iota(jnp.int32, (Q_BLOCK, Q_BLOCK), 0)
    s_idx = lax.broadcasted_iota(jnp.int32, (Q_BLOCK, Q_BLOCK), 1)
    tri = (s_idx <= t_idx).astype(F32)
    lane = lax.broadcasted_iota(jnp.int32, (Q_BLOCK, LANES), 1)
    pad_key = jnp.where(lane == N_F_PIECES, MASK_VALUE, 0.0)
    one_lane = jnp.where(lane == N_F_PIECES * n_heads, 1.0, 0.0)
    carry = carry_ref[...]
    for sub in range(block_rows // Q_BLOCK):
        rows = pl.ds(sub * Q_BLOCK, Q_BLOCK)
        row = i * block_rows + sub * Q_BLOCK + lax.broadcasted_iota(jnp.int32, (Q_BLOCK, 1), 0)
        real = row >= PAD_ROWS
        x = jnp.where(real, lf_ref[0, rows, :], 0.0)
        c = jnp.dot(tri, x, precision=lax.Precision.HIGHEST, preferred_element_type=F32) + carry
        carry = c[Q_BLOCK - 1:Q_BLOCK, :]
        f = jnp.where(lane < n_heads, c * LOG2E, 0.0)
        packed = one_lane
        for n in range(N_F_PIECES):
            piece = f.astype(BF16).astype(F32)
            f = f - piece
            packed = packed + (pltpu.roll(piece, n * n_heads, axis=1) if n else piece)
        packed = packed.astype(BF16)
        qx = jnp.dot(packed, selq_ref[...], preferred_element_type=F32)
        kx = jnp.dot(packed, selk_ref[...], preferred_element_type=F32)
        for h in range(n_heads):
            head = slice(h * LANES, (h + 1) * LANES)
            qx_ref[0, h, rows, :] = qx[:, head].astype(BF16)
            kx_ref[0, h, rows, :] = jnp.where(real, kx[:, head], pad_key).astype(BF16)
    carry_ref[...] = carry


def _forget_features(logf, n_heads):
    bsz, lp, _ = logf.shape
    block_rows = _pick_tile(lp, (384, 128))
    sel_q, sel_k = _feature_scatter(n_heads)
    out = jax.ShapeDtypeStruct((bsz, n_heads, lp, LANES), BF16)
    spec = pl.BlockSpec((1, n_heads, block_rows, LANES), lambda b, i: (b, 0, i, 0))
    sel_spec = pl.BlockSpec(sel_q.shape, lambda b, i: (0, 0))
    return pl.pallas_call(
        functools.partial(_forget_features_kernel, n_heads=n_heads, block_rows=block_rows),
        grid=(bsz, lp // block_rows),
        in_specs=[pl.BlockSpec((1, block_rows, LANES), lambda b, i: (b, i, 0)), sel_spec, sel_spec],
        out_specs=[spec, spec],
        out_shape=[out, out],
        scratch_shapes=[pltpu.VMEM((1, LANES), F32)],
        compiler_params=_params(("parallel", "arbitrary")),
        name="forget_features",
    )(logf, sel_q, sel_k)


SCORE_LOOKAHEAD = 2


def _fox_attention_kernel(q_ref, qx_ref, k_ref, kx_ref, v_ref, gate_ref, o_ref, *, tile):
    nt_dims = (((1,), (1,)), ((), ()))
    n_tiles = q_ref.shape[2] // tile
    row = lax.broadcasted_iota(jnp.int32, (tile, tile), 0)
    col = lax.broadcasted_iota(jnp.int32, (tile, tile), 1)
    causal = col <= row

    def rows(t):
        return pl.ds(t * tile, tile)

    def update(carry, s, j):
        m_old, l, acc = carry
        blocks = [s[:, c * LANES:(c + 1) * LANES] for c in range(tile // LANES)]
        row_max = jnp.max(functools.reduce(jnp.maximum, blocks), axis=-1, keepdims=True)
        m_new = jnp.maximum(m_old, row_max)
        alpha = jnp.exp2(m_old - m_new)
        p_blocks = [jnp.exp2(blk - m_new) for blk in blocks]
        l = alpha * l + functools.reduce(jnp.add, p_blocks)
        p = jnp.concatenate(p_blocks, axis=1).astype(BF16)
        acc = alpha * acc + jnp.dot(p, v_ref[0, 0, rows(j), :], preferred_element_type=F32)
        return m_new, l, acc

    def scores(qi, j):
        q = jnp.concatenate([q_ref[0, 0, rows(qi), :], qx_ref[0, 0, rows(qi), :]], axis=1)
        kt = jnp.concatenate([k_ref[0, 0, rows(j), :], kx_ref[0, 0, rows(j), :]], axis=1)
        s = lax.dot_general(q, kt, nt_dims, preferred_element_type=F32)
        return jnp.where(causal, s, MASK_VALUE) if j == qi else s

    pairs = [(qi, j) for qi in range(n_tiles) for j in range(qi + 1)]
    pending = [scores(*pair) for pair in pairs[:SCORE_LOOKAHEAD]]
    carry = None
    for n, (qi, j) in enumerate(pairs):
        if n + SCORE_LOOKAHEAD < len(pairs):
            pending.append(scores(*pairs[n + SCORE_LOOKAHEAD]))
        if j == 0:
            carry = (jnp.full((tile, LANES), -jnp.inf, F32), jnp.zeros((tile, LANES), F32),
                     jnp.zeros((tile, HEAD_DIM), F32))
        carry = update(carry, pending.pop(0), j)
        if j == qi:
            _, l, acc = carry
            l = jnp.sum(l, axis=-1, keepdims=True)
            out = ((acc / l) * gate_ref[0, 0, rows(qi), :]).astype(o_ref.dtype)
            first = qi * tile
            skip = max(HEAD_ROWS - first, 0)
            if skip < tile:
                o_ref[0, pl.ds(first + skip - HEAD_ROWS, tile - skip), :] = out[skip:]


def _fox_attention(q, qx, k, kx, v, gate):
    bsz, n_heads, lp, _ = q.shape
    tile = _pick_tile(lp, (384, 128))
    spec = pl.BlockSpec((1, 1, lp, HEAD_DIM), lambda b, h: (b, h, 0, 0))
    return pl.pallas_call(
        functools.partial(_fox_attention_kernel, tile=tile),
        grid=(bsz, n_heads),
        in_specs=[spec] * 6,
        out_specs=pl.BlockSpec((1, lp - HEAD_ROWS, HEAD_DIM), lambda b, h: (b, 0, h)),
        out_shape=jax.ShapeDtypeStruct((bsz, lp - HEAD_ROWS, n_heads * HEAD_DIM), BF16),
        compiler_params=_params(("parallel", "parallel")),
        name="fox_attention",
    )(q, qx, k, kx, v, gate)


def _out_proj_final_kernel(a_ref, w_ref, r_ref, o_ref):
    o_ref[0] = r_ref[0] + jnp.dot(a_ref[0], w_ref[...], preferred_element_type=F32)


def _out_proj_final(a, w, h):
    bsz, seq, d = a.shape
    tm = _pick_tile(seq, (512, 256, 128))
    return pl.pallas_call(
        _out_proj_final_kernel,
        grid=(bsz, seq // tm),
        in_specs=[pl.BlockSpec((1, tm, d), lambda b, i: (b, i, 0)),
                  pl.BlockSpec((d, d), lambda b, i: (0, 0)),
                  pl.BlockSpec((pl.Element(1), pl.Element(tm), pl.Element(d)),
                               lambda b, i: (b, pl.multiple_of(HEAD_ROWS + i * tm, HEAD_ROWS), 0))],
        out_specs=pl.BlockSpec((1, tm, d), lambda b, i: (b, i, 0)),
        out_shape=jax.ShapeDtypeStruct((bsz, seq, d), F32),
        compiler_params=_params(("parallel", "parallel")),
        name="out_proj_final",
    )(a, w, h)


def kernel(x, meta, gamma_lb, a_norm, a_w_in, a_out_norm, a_w_out, kv_norm, kv_w, fox_b_f, fox_k_norm,
           b_norm, b_w_in, b_q_norm, b_w_out):
    bsz, seq, d = x.shape
    n_heads = d // HEAD_DIM
    lp = HEAD_ROWS + seq
    assert a_norm.shape[0] == 1 and b_norm.shape[0] == 1, "one HGRN2 layer followed by one attention layer"

    h0 = jnp.concatenate([
        jnp.zeros((bsz, PAD_ROWS, d), x.dtype),
        jnp.broadcast_to(meta[None].astype(x.dtype), (bsz, N_META, d)),
        x], axis=1)

    g = _hgrn2_layer(h0, a_norm[0], a_w_in[0], gamma_lb, a_out_norm[0])

    w_f = jnp.zeros((d, LANES), BF16).at[:, :n_heads].set(kv_w[:, 2 * d:].astype(BF16))
    b_f = jnp.zeros((1, LANES), F32).at[0, :n_heads].set(fox_b_f.astype(F32))
    h1, x_kv, x_b, logf = _out_proj_norms(g.reshape(bsz * lp, d), a_w_out[0].astype(BF16),
                                          h0.reshape(bsz * lp, d), kv_norm, b_norm[0], w_f, b_f)
    x_kv = x_kv.reshape(bsz, lp, d)
    x_b = x_b.reshape(bsz, lp, d)

    kv_w16 = kv_w.astype(BF16)
    k_s = _head_proj(x_kv, kv_w16, 0, mode="headnorm", out_dtype=BF16, head_gain=fox_k_norm)
    v_s = _head_proj(x_kv, kv_w16, 1, mode="raw", out_dtype=BF16)
    qx, kx = _forget_features(logf.reshape(bsz, lp, LANES), n_heads)
    b_w16 = b_w_in[0].astype(BF16)
    q = _head_proj(x_b, b_w16, 0, mode="headnorm", out_dtype=BF16,
                   head_gain=b_q_norm[0], scale=HEAD_DIM ** -0.5 * LOG2E)
    gate = _head_proj(x_b, b_w16, 1, mode="silu", out_dtype=F32)
    o = _fox_attention(q, qx, k_s, kx, v_s, gate)
    return _out_proj_final(o, b_w_out[0].astype(BF16), h1.reshape(bsz, lp, d))
```

```python
import functools

import jax
import jax.numpy as jnp
from jax import lax
from jax.experimental import pallas as pl
from jax.experimental.pallas import tpu as pltpu

F32 = jnp.float32
BF16 = jnp.bfloat16

HEAD_DIM = 128
N_META = 16
Q_BLOCK = 128
PAD_ROWS = Q_BLOCK - N_META
HEAD_ROWS = PAD_ROWS + N_META
CHUNK = 64
EPS = 1e-6
MASK_VALUE = -1e30
LANES = 128
LOG2E = 1.4426950408889634
VMEM_LIMIT_BYTES = 60 * 1024 * 1024


def _pick_tile(n, candidates):
    for c in candidates:
        if n % c == 0:
            return c
    raise ValueError(f"no tile in {candidates} divides {n}")


def _params(semantics):
    return pltpu.CompilerParams(dimension_semantics=semantics, vmem_limit_bytes=VMEM_LIMIT_BYTES)


def _sigmoid(x):
    return 1.0 / (1.0 + jnp.exp(-x))


def _rms(x):
    return x * lax.rsqrt(jnp.mean(x * x, axis=-1, keepdims=True) + EPS)


HEADS_PER_STEP = 2
SECTIONS = 4
PROJ_CHUNKS_AT_STAGE = (3, 1, 1, 3)
PROJ_K_CHUNKS = sum(PROJ_CHUNKS_AT_STAGE)


def _hgrn2_head(u_ref, lb, g_out, st, first_row, side_work):
    rows = u_ref.shape[0]
    n_chunks = rows // CHUNK
    t_idx = lax.broadcasted_iota(jnp.int32, (CHUNK, CHUNK), 0)
    s_idx = lax.broadcasted_iota(jnp.int32, (CHUNK, CHUNK), 1)
    causal = s_idx <= t_idx
    nt_dims = (((1,), (1,)), ((), ()))
    tn_dims = (((0,), (0,)), ((), ()))

    def chunk(a, c):
        return a[c * CHUNK:(c + 1) * CHUNK]

    side_work[0]()
    q_raw = u_ref[:, 0 * HEAD_DIM:1 * HEAD_DIM]
    f_raw = u_ref[:, 1 * HEAD_DIM:2 * HEAD_DIM]
    valid = (first_row + lax.broadcasted_iota(jnp.int32, (rows, 1), 0)) >= PAD_ROWS
    q = q_raw * _sigmoid(q_raw)
    fg = lb + (1.0 - lb) * _sigmoid(f_raw)
    logf = jnp.where(valid, jnp.log(fg), 0.0)
    k = jnp.where(valid, 1.0 - fg, 0.0)
    v_b = u_ref[:, 2 * HEAD_DIM:3 * HEAD_DIM].astype(BF16)

    b_all = logf
    row_in_chunk = lax.broadcasted_iota(jnp.int32, (rows, 1), 0) % CHUNK
    step = 1
    while step < CHUNK:
        b_all = b_all + jnp.where(row_in_chunk >= step, pltpu.roll(b_all, step, axis=0), 0.0)
        step *= 2
    b = [chunk(b_all, c) for c in range(n_chunks)]

    q_intra, k_intra, q_inter, k_dec, decay = [], [], [], [], []
    for c in range(n_chunks):
        b_mid = b[c][CHUNK // 2 - 1:CHUNK // 2, :]
        b_last = b[c][CHUNK - 1:CHUNK, :]
        q_c, k_c = chunk(q, c), chunk(k, c)
        q_intra.append((q_c * jnp.exp(b[c] - b_mid)).astype(BF16))
        k_intra.append((k_c * jnp.exp(b_mid - b[c])).astype(BF16))
        q_inter.append((q_c * jnp.exp(b[c])).astype(BF16))
        k_dec.append((k_c * jnp.exp(b_last - b[c])).astype(BF16))
        decay.append(jnp.exp(b_last))
    a = [lax.dot_general(q_intra[c], k_intra[c], nt_dims, preferred_element_type=F32) for c in range(n_chunks)]
    side_work[1]()

    a = [jnp.where(causal, a_c, 0.0).astype(BF16) for a_c in a]
    o_intra = [jnp.dot(a[c], chunk(v_b, c), preferred_element_type=F32) for c in range(n_chunks)]
    d_st = [lax.dot_general(chunk(v_b, c), k_dec[c], tn_dims, preferred_element_type=F32)
            for c in range(n_chunks)]
    side_work[2]()

    st_before = []
    for c in range(n_chunks):
        st_before.append(st.astype(BF16))
        st = decay[c] * st + d_st[c]
    o = jnp.concatenate(
        [o_intra[c] + lax.dot_general(q_inter[c], st_before[c], nt_dims, preferred_element_type=F32)
         for c in range(n_chunks)], axis=0)
    side_work[3]()

    z = u_ref[:, 3 * HEAD_DIM:4 * HEAD_DIM]
    return _rms(o) * g_out * (z * _sigmoid(z)), st


def _hgrn2_layer_kernel(x_ref, g_ref, w_first_ref, w_odd_ref, w_even_ref, gam_ref, go_ref, o_ref,
                        xn_ref, u_even_ref, u_odd_ref, st_ref, *, rows_per_step):
    r = pl.program_id(1)
    k = pl.program_id(2)
    last_k = pl.num_programs(2) - 1
    last_r = pl.num_programs(1) - 1
    d = xn_ref.shape[2]
    k_chunk = d // PROJ_K_CHUNKS
    this_tile = r % 2
    next_tile = (r + 1) % 2
    prefetch_next_tile = (k == last_k) & (r < last_r)

    def normalise(slot):
        xn_ref[slot] = (_rms(x_ref[0]) * g_ref[...]).astype(BF16)

    @pl.when((r == 0) & (k == 0))
    def _():
        st_ref[...] = jnp.zeros_like(st_ref)
        normalise(0)
        u_even_ref[...] = jnp.dot(xn_ref[0], w_first_ref[0], preferred_element_type=F32)

    @pl.when(prefetch_next_tile)
    def _():
        normalise(next_tile)

    gam = gam_ref[...]
    e = jnp.exp(gam - jnp.max(gam, axis=0, keepdims=True))
    lb = e[0:1] / jnp.sum(e, axis=0, keepdims=True)
    g_out = go_ref[...]
    first_row = r * rows_per_step

    def head(local, u_ref, u_next_ref, w_next_ref, xn_slot):
        acc = []

        def project_chunks(first, count):
            def work():
                for c in range(first, first + count):
                    cols = pl.ds(c * k_chunk, k_chunk)
                    part = jnp.dot(xn_ref[xn_slot, :, cols], w_next_ref[0, cols, :], preferred_element_type=F32)
                    acc[:] = [part if not acc else acc[0] + part]
            return work

        starts = [sum(PROJ_CHUNKS_AT_STAGE[:n]) for n in range(len(PROJ_CHUNKS_AT_STAGE))]
        lanes = slice(local * HEAD_DIM, (local + 1) * HEAD_DIM)
        idx = HEADS_PER_STEP * k + local
        out, st = _hgrn2_head(u_ref, lb[:, lanes], g_out[:, lanes], st_ref[idx], first_row,
                              [project_chunks(s, n) for s, n in zip(starts, PROJ_CHUNKS_AT_STAGE)])
        u_next_ref[...] = acc[0]
        st_ref[idx] = st
        o_ref[0, :, lanes] = out.astype(o_ref.dtype)

    head(0, u_even_ref, u_odd_ref, w_odd_ref, this_tile)
    head(1, u_odd_ref, u_even_ref, w_even_ref, jnp.where(prefetch_next_tile, next_tile, this_tile))


def _head_major_weights_kernel(*refs):
    *section_refs, o_ref = refs
    o_ref[0] = jnp.concatenate([ref[...] for ref in section_refs], axis=1).astype(o_ref.dtype)


def _head_major_weights(w_in, n_heads):
    d = w_in.shape[0]

    def section(s):
        return pl.BlockSpec((d, HEAD_DIM), lambda n, s=s: (0, s * n_heads + n))

    return pl.pallas_call(
        _head_major_weights_kernel,
        grid=(n_heads,),
        in_specs=[section(s) for s in range(SECTIONS)],
        out_specs=pl.BlockSpec((1, d, SECTIONS * HEAD_DIM), lambda n: (n, 0, 0)),
        out_shape=jax.ShapeDtypeStruct((n_heads, d, SECTIONS * HEAD_DIM), BF16),
        compiler_params=_params(("parallel",)),
        name="head_major_weights",
    )(*([w_in] * SECTIONS))


def _hgrn2_layer(h, g_norm, w_in, gamma_lb, g_out):
    bsz, lp, d = h.shape
    n_heads = d // HEAD_DIM
    n_pairs = n_heads // HEADS_PER_STEP
    assert n_heads % HEADS_PER_STEP == 0 and n_pairs >= 2 and d % PROJ_K_CHUNKS == 0
    rows = _pick_tile(lp, (704, 384, 128, 64))
    n_row_tiles = lp // rows
    n_lb = gamma_lb.shape[0]
    width = HEADS_PER_STEP * HEAD_DIM
    w_heads = _head_major_weights(w_in, n_heads)
    w_block = (1, d, SECTIONS * HEAD_DIM)

    def x_tile(b, r, k):
        return b, jnp.where(k == n_pairs - 1, jnp.minimum(r + 1, n_row_tiles - 1), r), 0

    return pl.pallas_call(
        functools.partial(_hgrn2_layer_kernel, rows_per_step=rows),
        grid=(bsz, n_row_tiles, n_pairs),
        in_specs=[pl.BlockSpec((1, rows, d), x_tile),
                  pl.BlockSpec((1, d), lambda b, r, k: (0, 0)),
                  pl.BlockSpec(w_block, lambda b, r, k: (0, 0, 0)),
                  pl.BlockSpec(w_block, lambda b, r, k: (HEADS_PER_STEP * k + 1, 0, 0)),
                  pl.BlockSpec(w_block, lambda b, r, k: ((HEADS_PER_STEP * k + 2) % n_heads, 0, 0)),
                  pl.BlockSpec((n_lb, width), lambda b, r, k: (0, k)),
                  pl.BlockSpec((1, width), lambda b, r, k: (0, k))],
        out_specs=pl.BlockSpec((1, rows, width), lambda b, r, k: (b, r, k)),
        out_shape=jax.ShapeDtypeStruct((bsz, lp, d), BF16),
        scratch_shapes=[pltpu.VMEM((2, rows, d), BF16),
                        pltpu.VMEM((rows, SECTIONS * HEAD_DIM), F32),
                        pltpu.VMEM((rows, SECTIONS * HEAD_DIM), F32),
                        pltpu.VMEM((n_heads, HEAD_DIM, HEAD_DIM), F32)],
        compiler_params=_params(("parallel", "arbitrary", "arbitrary")),
        name="hgrn2_layer",
    )(h, g_norm.reshape(1, d).astype(F32), w_heads, w_heads, w_heads,
      gamma_lb.astype(F32), g_out.reshape(1, d).astype(F32))


OUT_PROJ_SUB_ROWS = (176, 128)


def _out_proj_norms_kernel(a_ref, w_ref, r_ref, gkv_ref, gb_ref, wf_ref, bf_ref,
                           h_ref, xkv_ref, xb_ref, lf_ref):
    tm = a_ref.shape[0]
    sub = _pick_tile(tm, OUT_PROJ_SUB_ROWS)

    def product(c):
        return jnp.dot(a_ref[pl.ds(c * sub, sub), :], w_ref[...], preferred_element_type=F32)

    acc = product(0)
    for c in range(tm // sub):
        acc_next = product(c + 1) if c + 1 < tm // sub else None
        rows = pl.ds(c * sub, sub)
        h = r_ref[rows, :] + acc
        h_ref[rows, :] = h
        y = _rms(h)
        x_kv = (y * gkv_ref[...]).astype(BF16)
        xkv_ref[rows, :] = x_kv
        xb_ref[rows, :] = (y * gb_ref[...]).astype(BF16)
        fl = jnp.dot(x_kv, wf_ref[...], preferred_element_type=F32) + bf_ref[...]
        lf_ref[rows, :] = jnp.minimum(fl, 0.0) - jnp.log(1.0 + jnp.exp(-jnp.abs(fl)))
        acc = acc_next


def _out_proj_norms(a, w, res, g_kv, g_b, w_f, b_f):
    m, d = a.shape
    tm = _pick_tile(m, (528, 384, 256, 128))
    row_spec = pl.BlockSpec((tm, d), lambda i: (i, 0))
    vec_spec = pl.BlockSpec((1, d), lambda i: (0, 0))
    return pl.pallas_call(
        _out_proj_norms_kernel,
        grid=(m // tm,),
        in_specs=[row_spec, pl.BlockSpec((d, d), lambda i: (0, 0)), row_spec, vec_spec, vec_spec,
                  pl.BlockSpec((d, LANES), lambda i: (0, 0)), pl.BlockSpec((1, LANES), lambda i: (0, 0))],
        out_specs=[row_spec, row_spec, row_spec, pl.BlockSpec((tm, LANES), lambda i: (i, 0))],
        out_shape=[jax.ShapeDtypeStruct((m, d), F32), jax.ShapeDtypeStruct((m, d), BF16),
                   jax.ShapeDtypeStruct((m, d), BF16), jax.ShapeDtypeStruct((m, LANES), F32)],
        compiler_params=_params(("parallel",)),
        name="out_proj_norms",
    )(a, w, res, g_kv.reshape(1, d).astype(F32), g_b.reshape(1, d).astype(F32), w_f, b_f)


PROJ_COLS_PER_DOT = 512


def _head_proj_kernel(*refs, mode, scale):
    if mode == "headnorm":
        x_ref, w_ref, hg_ref, o_ref = refs
    else:
        x_ref, w_ref, o_ref = refs
    x = x_ref[0]
    n_out = w_ref.shape[1]
    cols_per_dot = min(PROJ_COLS_PER_DOT, n_out)
    assert n_out % cols_per_dot == 0
    heads_per_dot = cols_per_dot // HEAD_DIM
    for c in range(n_out // cols_per_dot):
        cols = pl.ds(c * cols_per_dot, cols_per_dot)
        acc = jnp.dot(x, w_ref[:, cols], preferred_element_type=F32)
        for n in range(heads_per_dot):
            a = acc[:, n * HEAD_DIM:(n + 1) * HEAD_DIM]
            head = c * heads_per_dot + n
            if mode == "headnorm":
                a = _rms(a) * hg_ref[:, head * HEAD_DIM:(head + 1) * HEAD_DIM]
                if scale is not None:
                    a = a * scale
            elif mode == "silu":
                a = a * _sigmoid(a)
            o_ref[0, head] = a.astype(o_ref.dtype)


def _head_proj(x, w, section, *, mode, out_dtype, head_gain=None, scale=None):
    bsz, lp, d = x.shape
    n_out = d
    tm = _pick_tile(lp, (1056, 704, 384, 128))
    in_specs = [pl.BlockSpec((1, tm, d), lambda b, i: (b, i, 0)),
                pl.BlockSpec((d, n_out), lambda b, i: (0, section))]
    args = [x, w]
    if mode == "headnorm":
        in_specs.append(pl.BlockSpec((1, n_out), lambda b, i: (0, 0)))
        args.append(head_gain.reshape(1, n_out).astype(F32))
    return pl.pallas_call(
        functools.partial(_head_proj_kernel, mode=mode, scale=scale),
        grid=(bsz, lp // tm),
        in_specs=in_specs,
        out_specs=pl.BlockSpec((1, n_out // HEAD_DIM, tm, HEAD_DIM), lambda b, i: (b, 0, i, 0)),
        out_shape=jax.ShapeDtypeStruct((bsz, n_out // HEAD_DIM, lp, HEAD_DIM), out_dtype),
        compiler_params=_params(("parallel", "parallel")),
        name=f"head_proj_{mode}",
    )(*args)


N_F_PIECES = 3


def _feature_scatter(n_heads):
    assert N_F_PIECES * n_heads < LANES
    r = jnp.arange(LANES)[:, None]
    c = jnp.arange(n_heads * LANES)[None, :]
    piece, src = r // n_heads, r % n_heads
    head, lane = c // LANES, c % LANES
    is_piece = (piece < N_F_PIECES) & (src == head)
    is_one = r == N_F_PIECES * n_heads
    sel_q = (is_piece & (lane == piece)) | (is_one & (lane >= N_F_PIECES) & (lane < 2 * N_F_PIECES))
    sel_k = (is_one & (lane < N_F_PIECES)).astype(F32) - (is_piece & (lane == N_F_PIECES + piece)).astype(F32)
    return sel_q.astype(BF16), sel_k.astype(BF16)


def _forget_features_kernel(lf_ref, selq_ref, selk_ref, qx_ref, kx_ref, carry_ref, *, n_heads, block_rows):
    i = pl.program_id(1)

    @pl.when(i == 0)
    def _():
        carry_ref[...] = jnp.zeros_like(carry_ref)

    t_idx = lax.broadcasted_iota(jnp.int32, (Q_BLOCK, Q_BLOCK), 0)
    s_idx = lax.broadcasted_iota(jnp.int32, (Q_BLOCK, Q_BLOCK), 1)
    tri = (s_idx <= t_idx).astype(F32)
    lane = lax.broadcasted_iota(jnp.int32, (Q_BLOCK, LANES), 1)
    pad_key = jnp.where(lane == N_F_PIECES, MASK_VALUE, 0.0)
    one_lane = jnp.where(lane == N_F_PIECES * n_heads, 1.0, 0.0)
    carry = carry_ref[...]
    for sub in range(block_rows // Q_BLOCK):
        rows = pl.ds(sub * Q_BLOCK, Q_BLOCK)
        row = i * block_rows + sub * Q_BLOCK + lax.broadcasted_iota(jnp.int32, (Q_BLOCK, 1), 0)
        real = row >= PAD_ROWS
        x = jnp.where(real, lf_ref[0, rows, :], 0.0)
        c = jnp.dot(tri, x, precision=lax.Precision.HIGHEST, preferred_element_type=F32) + carry
        carry = c[Q_BLOCK - 1:Q_BLOCK, :]
        f = jnp.where(lane < n_heads, c * LOG2E, 0.0)
        packed = one_lane
        for n in range(N_F_PIECES):
            piece = f.astype(BF16).astype(F32)
            f = f - piece
            packed = packed + (pltpu.roll(piece, n * n_heads, axis=1) if n else piece)
        packed = packed.astype(BF16)
        qx = jnp.dot(packed, selq_ref[...], preferred_element_type=F32)
        kx = jnp.dot(packed, selk_ref[...], preferred_element_type=F32)
        for h in range(n_heads):
            head = slice(h * LANES, (h + 1) * LANES)
            qx_ref[0, h, rows, :] = qx[:, head].astype(BF16)
            kx_ref[0, h, rows, :] = jnp.where(real, kx[:, head], pad_key).astype(BF16)
    carry_ref[...] = carry


def _forget_features(logf, n_heads):
    bsz, lp, _ = logf.shape
    block_rows = _pick_tile(lp, (384, 128))
    sel_q, sel_k = _feature_scatter(n_heads)
    out = jax.ShapeDtypeStruct((bsz, n_heads, lp, LANES), BF16)
    spec = pl.BlockSpec((1, n_heads, block_rows, LANES), lambda b, i: (b, 0, i, 0))
    sel_spec = pl.BlockSpec(sel_q.shape, lambda b, i: (0, 0))
    return pl.pallas_call(
        functools.partial(_forget_features_kernel, n_heads=n_heads, block_rows=block_rows),
        grid=(bsz, lp // block_rows),
        in_specs=[pl.BlockSpec((1, block_rows, LANES), lambda b, i: (b, i, 0)), sel_spec, sel_spec],
        out_specs=[spec, spec],
        out_shape=[out, out],
        scratch_shapes=[pltpu.VMEM((1, LANES), F32)],
        compiler_params=_params(("parallel", "arbitrary")),
        name="forget_features",
    )(logf, sel_q, sel_k)


SCORE_LOOKAHEAD = 2
KEY_TILE = 256


def _key_tiles(q_first, q_rows):
    tiles, start, end = [], 0, q_first + q_rows
    while start < end:
        width = min(KEY_TILE, end - start)
        masked = start + width - 1 > q_first
        tiles.append((start, width, q_first - start if masked else None))
        start += width
    return tiles


def _fox_attention_kernel(q_ref, qx_ref, k_ref, kx_ref, v_ref, gate_ref, o_ref, *, tile):
    nt_dims = (((1,), (1,)), ((), ()))
    n_tiles = q_ref.shape[2] // tile
    col_minus_row = (lax.broadcasted_iota(jnp.int32, (tile, KEY_TILE), 1)
                     - lax.broadcasted_iota(jnp.int32, (tile, KEY_TILE), 0))

    def update(carry, s, start, width):
        m_old, l, acc = carry
        blocks = [s[:, c * LANES:(c + 1) * LANES] for c in range(width // LANES)]
        row_max = jnp.max(functools.reduce(jnp.maximum, blocks), axis=-1, keepdims=True)
        m_new = jnp.maximum(m_old, row_max)
        alpha = jnp.exp2(m_old - m_new)
        p_blocks = [jnp.exp2(blk - m_new) for blk in blocks]
        l = alpha * l + functools.reduce(jnp.add, p_blocks)
        p = jnp.concatenate(p_blocks, axis=1).astype(BF16)
        acc = alpha * acc + jnp.dot(p, v_ref[0, 0, pl.ds(start, width), :], preferred_element_type=F32)
        return m_new, l, acc

    def scores(qi, start, width, offset):
        q_rows = pl.ds(qi * tile, tile)
        k_rows = pl.ds(start, width)
        q = jnp.concatenate([q_ref[0, 0, q_rows, :], qx_ref[0, 0, q_rows, :]], axis=1)
        kt = jnp.concatenate([k_ref[0, 0, k_rows, :], kx_ref[0, 0, k_rows, :]], axis=1)
        s = lax.dot_general(q, kt, nt_dims, preferred_element_type=F32)
        if offset is None:
            return s
        return jnp.where(col_minus_row[:, :width] <= offset, s, MASK_VALUE)

    pairs = []
    for qi in range(n_tiles):
        tiles = _key_tiles(qi * tile, tile)
        pairs += [(qi, n == 0, n == len(tiles) - 1, t) for n, t in enumerate(tiles)]
    pending = [scores(qi, *t) for qi, _, _, t in pairs[:SCORE_LOOKAHEAD]]
    carry = None
    for n, (qi, is_first, is_last, (start, width, _)) in enumerate(pairs):
        if n + SCORE_LOOKAHEAD < len(pairs):
            nxt = pairs[n + SCORE_LOOKAHEAD]
            pending.append(scores(nxt[0], *nxt[3]))
        if is_first:
            carry = (jnp.full((tile, LANES), -jnp.inf, F32), jnp.zeros((tile, LANES), F32),
                     jnp.zeros((tile, HEAD_DIM), F32))
        carry = update(carry, pending.pop(0), start, width)
        if is_last:
            _, l, acc = carry
            l = jnp.sum(l, axis=-1, keepdims=True)
            q_rows = pl.ds(qi * tile, tile)
            out = ((acc / l) * gate_ref[0, 0, q_rows, :]).astype(o_ref.dtype)
            first = qi * tile
            skip = max(HEAD_ROWS - first, 0)
            if skip < tile:
                o_ref[0, pl.ds(first + skip - HEAD_ROWS, tile - skip), :] = out[skip:]


def _fox_attention(q, qx, k, kx, v, gate):
    bsz, n_heads, lp, _ = q.shape
    tile = _pick_tile(lp, (384, 128))
    spec = pl.BlockSpec((1, 1, lp, HEAD_DIM), lambda b, h: (b, h, 0, 0))
    return pl.pallas_call(
        functools.partial(_fox_attention_kernel, tile=tile),
        grid=(bsz, n_heads),
        in_specs=[spec] * 6,
        out_specs=pl.BlockSpec((1, lp - HEAD_ROWS, HEAD_DIM), lambda b, h: (b, 0, h)),
        out_shape=jax.ShapeDtypeStruct((bsz, lp - HEAD_ROWS, n_heads * HEAD_DIM), BF16),
        compiler_params=_params(("parallel", "parallel")),
        name="fox_attention",
    )(q, qx, k, kx, v, gate)


def _out_proj_final_kernel(a_ref, w_ref, r_ref, o_ref):
    o_ref[0] = r_ref[0] + jnp.dot(a_ref[0], w_ref[...], preferred_element_type=F32)


def _out_proj_final(a, w, h):
    bsz, seq, d = a.shape
    tm = _pick_tile(seq, (512, 256, 128))
    return pl.pallas_call(
        _out_proj_final_kernel,
        grid=(bsz, seq // tm),
        in_specs=[pl.BlockSpec((1, tm, d), lambda b, i: (b, i, 0)),
                  pl.BlockSpec((d, d), lambda b, i: (0, 0)),
                  pl.BlockSpec((pl.Element(1), pl.Element(tm), pl.Element(d)),
                               lambda b, i: (b, pl.multiple_of(HEAD_ROWS + i * tm, HEAD_ROWS), 0))],
        out_specs=pl.BlockSpec((1, tm, d), lambda b, i: (b, i, 0)),
        out_shape=jax.ShapeDtypeStruct((bsz, seq, d), F32),
        compiler_params=_params(("parallel", "parallel")),
        name="out_proj_final",
    )(a, w, h)


def kernel(x, meta, gamma_lb, a_norm, a_w_in, a_out_norm, a_w_out, kv_norm, kv_w, fox_b_f, fox_k_norm,
           b_norm, b_w_in, b_q_norm, b_w_out):
    bsz, seq, d = x.shape
    n_heads = d // HEAD_DIM
    lp = HEAD_ROWS + seq
    assert a_norm.shape[0] == 1 and b_norm.shape[0] == 1, "one HGRN2 layer followed by one attention layer"

    h0 = jnp.concatenate([
        jnp.zeros((bsz, PAD_ROWS, d), x.dtype),
        jnp.broadcast_to(meta[None].astype(x.dtype), (bsz, N_META, d)),
        x], axis=1)

    g = _hgrn2_layer(h0, a_norm[0], a_w_in[0], gamma_lb, a_out_norm[0])

    w_f = jnp.zeros((d, LANES), BF16).at[:, :n_heads].set(kv_w[:, 2 * d:].astype(BF16))
    b_f = jnp.zeros((1, LANES), F32).at[0, :n_heads].set(fox_b_f.astype(F32))
    h1, x_kv, x_b, logf = _out_proj_norms(g.reshape(bsz * lp, d), a_w_out[0].astype(BF16),
                                          h0.reshape(bsz * lp, d), kv_norm, b_norm[0], w_f, b_f)
    x_kv = x_kv.reshape(bsz, lp, d)
    x_b = x_b.reshape(bsz, lp, d)

    kv_w16 = kv_w.astype(BF16)
    k_s = _head_proj(x_kv, kv_w16, 0, mode="headnorm", out_dtype=BF16, head_gain=fox_k_norm)
    v_s = _head_proj(x_kv, kv_w16, 1, mode="raw", out_dtype=BF16)
    qx, kx = _forget_features(logf.reshape(bsz, lp, LANES), n_heads)
    b_w16 = b_w_in[0].astype(BF16)
    q = _head_proj(x_b, b_w16, 0, mode="headnorm", out_dtype=BF16,
                   head_gain=b_q_norm[0], scale=HEAD_DIM ** -0.5 * LOG2E)
    gate = _head_proj(x_b, b_w16, 1, mode="silu", out_dtype=F32)
    o = _fox_attention(q, qx, k_s, kx, v_s, gate)
    return _out_proj_final(o, b_w_out[0].astype(BF16), h1.reshape(bsz, lp, d))
```

```python
import functools

import jax
import jax.numpy as jnp
from jax import lax
from jax.experimental import pallas as pl
from jax.experimental.pallas import tpu as pltpu

F32 = jnp.float32
BF16 = jnp.bfloat16

HEAD_DIM = 128
N_META = 16
Q_BLOCK = 128
PAD_ROWS = Q_BLOCK - N_META
HEAD_ROWS = PAD_ROWS + N_META
CHUNK = 64
EPS = 1e-6
MASK_VALUE = -1e30
LANES = 128
LOG2E = 1.4426950408889634
VMEM_LIMIT_BYTES = 60 * 1024 * 1024


def _pick_tile(n, candidates):
    for c in candidates:
        if n % c == 0:
            return c
    raise ValueError(f"no tile in {candidates} divides {n}")


def _params(semantics):
    return pltpu.CompilerParams(dimension_semantics=semantics, vmem_limit_bytes=VMEM_LIMIT_BYTES)


def _sigmoid(x):
    return 1.0 / (1.0 + jnp.exp(-x))


def _rms(x):
    return x * lax.rsqrt(jnp.mean(x * x, axis=-1, keepdims=True) + EPS)


HEADS_PER_STEP = 2
SECTIONS = 4
PROJ_CHUNKS_AT_STAGE = (3, 1, 1, 3)
PROJ_K_CHUNKS = sum(PROJ_CHUNKS_AT_STAGE)


def _hgrn2_head(u_ref, lb, g_out, st, first_row, side_work):
    rows = u_ref.shape[0]
    n_chunks = rows // CHUNK
    t_idx = lax.broadcasted_iota(jnp.int32, (CHUNK, CHUNK), 0)
    s_idx = lax.broadcasted_iota(jnp.int32, (CHUNK, CHUNK), 1)
    causal = s_idx <= t_idx
    nt_dims = (((1,), (1,)), ((), ()))
    tn_dims = (((0,), (0,)), ((), ()))

    def chunk(a, c):
        return a[c * CHUNK:(c + 1) * CHUNK]

    side_work[0]()
    q_raw = u_ref[:, 0 * HEAD_DIM:1 * HEAD_DIM]
    f_raw = u_ref[:, 1 * HEAD_DIM:2 * HEAD_DIM]
    valid = (first_row + lax.broadcasted_iota(jnp.int32, (rows, 1), 0)) >= PAD_ROWS
    q = q_raw * _sigmoid(q_raw)
    fg = lb + (1.0 - lb) * _sigmoid(f_raw)
    logf = jnp.where(valid, jnp.log(fg), 0.0)
    k = jnp.where(valid, 1.0 - fg, 0.0)
    v_b = u_ref[:, 2 * HEAD_DIM:3 * HEAD_DIM].astype(BF16)

    b_all = logf
    row_in_chunk = lax.broadcasted_iota(jnp.int32, (rows, 1), 0) % CHUNK
    step = 1
    while step < CHUNK:
        b_all = b_all + jnp.where(row_in_chunk >= step, pltpu.roll(b_all, step, axis=0), 0.0)
        step *= 2
    b = [chunk(b_all, c) for c in range(n_chunks)]

    q_intra, k_intra, q_inter, k_dec, decay = [], [], [], [], []
    for c in range(n_chunks):
        b_mid = b[c][CHUNK // 2 - 1:CHUNK // 2, :]
        b_last = b[c][CHUNK - 1:CHUNK, :]
        q_c, k_c = chunk(q, c), chunk(k, c)
        q_intra.append((q_c * jnp.exp(b[c] - b_mid)).astype(BF16))
        k_intra.append((k_c * jnp.exp(b_mid - b[c])).astype(BF16))
        q_inter.append((q_c * jnp.exp(b[c])).astype(BF16))
        k_dec.append((k_c * jnp.exp(b_last - b[c])).astype(BF16))
        decay.append(jnp.exp(b_last))
    a = [lax.dot_general(q_intra[c], k_intra[c], nt_dims, preferred_element_type=F32) for c in range(n_chunks)]
    side_work[1]()

    a = [jnp.where(causal, a_c, 0.0).astype(BF16) for a_c in a]
    o_intra = [jnp.dot(a[c], chunk(v_b, c), preferred_element_type=F32) for c in range(n_chunks)]
    d_st = [lax.dot_general(chunk(v_b, c), k_dec[c], tn_dims, preferred_element_type=F32)
            for c in range(n_chunks)]
    side_work[2]()

    st_before = []
    for c in range(n_chunks):
        st_before.append(st.astype(BF16))
        st = decay[c] * st + d_st[c]
    o = jnp.concatenate(
        [o_intra[c] + lax.dot_general(q_inter[c], st_before[c], nt_dims, preferred_element_type=F32)
         for c in range(n_chunks)], axis=0)
    side_work[3]()

    z = u_ref[:, 3 * HEAD_DIM:4 * HEAD_DIM]
    return _rms(o) * g_out * (z * _sigmoid(z)), st


def _hgrn2_layer_kernel(x_ref, head_ref, g_ref, w_first_ref, w_odd_ref, w_even_ref, gam_ref, go_ref, o_ref,
                        xn_ref, u_even_ref, u_odd_ref, st_ref, *, rows_per_step):
    r = pl.program_id(1)
    k = pl.program_id(2)
    last_k = pl.num_programs(2) - 1
    last_r = pl.num_programs(1) - 1
    d = xn_ref.shape[2]
    k_chunk = d // PROJ_K_CHUNKS
    this_tile = r % 2
    next_tile = (r + 1) % 2
    prefetch_next_tile = (k == last_k) & (r < last_r)

    def normalise(slot, first_tile):
        rows = x_ref[0]
        if first_tile:
            rows = jnp.concatenate([head_ref[...], rows[:rows_per_step - HEAD_ROWS]], axis=0)
        xn_ref[slot] = (_rms(rows) * g_ref[...]).astype(BF16)

    @pl.when((r == 0) & (k == 0))
    def _():
        st_ref[...] = jnp.zeros_like(st_ref)
        normalise(0, True)
        u_even_ref[...] = jnp.dot(xn_ref[0], w_first_ref[0], preferred_element_type=F32)

    @pl.when(prefetch_next_tile)
    def _():
        normalise(next_tile, False)

    gam = gam_ref[...]
    e = jnp.exp(gam - jnp.max(gam, axis=0, keepdims=True))
    lb = e[0:1] / jnp.sum(e, axis=0, keepdims=True)
    g_out = go_ref[...]
    first_row = r * rows_per_step

    def head(local, u_ref, u_next_ref, w_next_ref, xn_slot):
        acc = []

        def project_chunks(first, count):
            def work():
                for c in range(first, first + count):
                    cols = pl.ds(c * k_chunk, k_chunk)
                    part = jnp.dot(xn_ref[xn_slot, :, cols], w_next_ref[0, cols, :], preferred_element_type=F32)
                    acc[:] = [part if not acc else acc[0] + part]
            return work

        starts = [sum(PROJ_CHUNKS_AT_STAGE[:n]) for n in range(len(PROJ_CHUNKS_AT_STAGE))]
        lanes = slice(local * HEAD_DIM, (local + 1) * HEAD_DIM)
        idx = HEADS_PER_STEP * k + local
        out, st = _hgrn2_head(u_ref, lb[:, lanes], g_out[:, lanes], st_ref[idx], first_row,
                              [project_chunks(s, n) for s, n in zip(starts, PROJ_CHUNKS_AT_STAGE)])
        u_next_ref[...] = acc[0]
        st_ref[idx] = st
        o_ref[0, :, lanes] = out.astype(o_ref.dtype)

    head(0, u_even_ref, u_odd_ref, w_odd_ref, this_tile)
    head(1, u_odd_ref, u_even_ref, w_even_ref, jnp.where(prefetch_next_tile, next_tile, this_tile))


def _head_major_weights_kernel(*refs):
    *section_refs, o_ref = refs
    o_ref[0] = jnp.concatenate([ref[...] for ref in section_refs], axis=1).astype(o_ref.dtype)


def _head_major_weights(w_in, n_heads):
    d = w_in.shape[0]

    def section(s):
        return pl.BlockSpec((d, HEAD_DIM), lambda n, s=s: (0, s * n_heads + n))

    return pl.pallas_call(
        _head_major_weights_kernel,
        grid=(n_heads,),
        in_specs=[section(s) for s in range(SECTIONS)],
        out_specs=pl.BlockSpec((1, d, SECTIONS * HEAD_DIM), lambda n: (n, 0, 0)),
        out_shape=jax.ShapeDtypeStruct((n_heads, d, SECTIONS * HEAD_DIM), BF16),
        compiler_params=_params(("parallel",)),
        name="head_major_weights",
    )(*([w_in] * SECTIONS))


def _token_rows_spec(rows, d, tile_of):
    def index(b, *idx):
        first = jnp.maximum(tile_of(*idx) * rows - HEAD_ROWS, 0)
        return b, pl.multiple_of(first, 8), 0
    return pl.BlockSpec((pl.Element(1), pl.Element(rows), pl.Element(d)), index)


def _hgrn2_layer(x, head, g_norm, w_in, gamma_lb, g_out):
    bsz, seq, d = x.shape
    lp = HEAD_ROWS + seq
    n_heads = d // HEAD_DIM
    n_pairs = n_heads // HEADS_PER_STEP
    assert n_heads % HEADS_PER_STEP == 0 and n_pairs >= 2 and d % PROJ_K_CHUNKS == 0
    rows = _pick_tile(lp, (704, 384, 128, 64))
    n_row_tiles = lp // rows
    n_lb = gamma_lb.shape[0]
    width = HEADS_PER_STEP * HEAD_DIM
    w_heads = _head_major_weights(w_in, n_heads)
    w_block = (1, d, SECTIONS * HEAD_DIM)

    assert rows > HEAD_ROWS

    def x_tile(r, k):
        return jnp.where(k == n_pairs - 1, jnp.minimum(r + 1, n_row_tiles - 1), r)

    return pl.pallas_call(
        functools.partial(_hgrn2_layer_kernel, rows_per_step=rows),
        grid=(bsz, n_row_tiles, n_pairs),
        in_specs=[_token_rows_spec(rows, d, x_tile),
                  pl.BlockSpec((HEAD_ROWS, d), lambda b, r, k: (0, 0)),
                  pl.BlockSpec((1, d), lambda b, r, k: (0, 0)),
                  pl.BlockSpec(w_block, lambda b, r, k: (0, 0, 0)),
                  pl.BlockSpec(w_block, lambda b, r, k: (HEADS_PER_STEP * k + 1, 0, 0)),
                  pl.BlockSpec(w_block, lambda b, r, k: ((HEADS_PER_STEP * k + 2) % n_heads, 0, 0)),
                  pl.BlockSpec((n_lb, width), lambda b, r, k: (0, k)),
                  pl.BlockSpec((1, width), lambda b, r, k: (0, k))],
        out_specs=pl.BlockSpec((1, rows, width), lambda b, r, k: (b, r, k)),
        out_shape=jax.ShapeDtypeStruct((bsz, lp, d), BF16),
        scratch_shapes=[pltpu.VMEM((2, rows, d), BF16),
                        pltpu.VMEM((rows, SECTIONS * HEAD_DIM), F32),
                        pltpu.VMEM((rows, SECTIONS * HEAD_DIM), F32),
                        pltpu.VMEM((n_heads, HEAD_DIM, HEAD_DIM), F32)],
        compiler_params=_params(("parallel", "arbitrary", "arbitrary")),
        name="hgrn2_layer",
    )(x, head, g_norm.reshape(1, d).astype(F32), w_heads, w_heads, w_heads,
      gamma_lb.astype(F32), g_out.reshape(1, d).astype(F32))


OUT_PROJ_SUB_ROWS = (176, 128)


def _out_proj_norms_kernel(a_ref, w_ref, x_ref, head_ref, gkv_ref, gb_ref, wf_ref, bf_ref,
                           h_ref, xkv_ref, xb_ref, lf_ref):
    tm = a_ref.shape[1]
    sub = _pick_tile(tm, OUT_PROJ_SUB_ROWS)
    assert sub >= HEAD_ROWS

    def product(c):
        return jnp.dot(a_ref[0, pl.ds(c * sub, sub), :], w_ref[...], preferred_element_type=F32)

    def body(residual):
        acc = product(0)
        for c in range(tm // sub):
            acc_next = product(c + 1) if c + 1 < tm // sub else None
            rows = pl.ds(c * sub, sub)
            h = residual(c) + acc
            h_ref[0, rows, :] = h
            y = _rms(h)
            x_kv = (y * gkv_ref[...]).astype(BF16)
            xkv_ref[0, rows, :] = x_kv
            xb_ref[0, rows, :] = (y * gb_ref[...]).astype(BF16)
            fl = jnp.dot(x_kv, wf_ref[...], preferred_element_type=F32) + bf_ref[...]
            lf_ref[0, rows, :] = jnp.minimum(fl, 0.0) - jnp.log(1.0 + jnp.exp(-jnp.abs(fl)))
            acc = acc_next

    def first_tile_residual(c):
        if c == 0:
            return jnp.concatenate([head_ref[...], x_ref[0, pl.ds(0, sub - HEAD_ROWS), :]], axis=0)
        return x_ref[0, pl.ds(c * sub - HEAD_ROWS, sub), :]

    is_first_tile = pl.program_id(1) == 0
    pl.when(is_first_tile)(lambda: body(first_tile_residual))
    pl.when(jnp.logical_not(is_first_tile))(lambda: body(lambda c: x_ref[0, pl.ds(c * sub, sub), :]))


def _out_proj_norms(a, w, x, head, g_kv, g_b, w_f, b_f):
    bsz, lp, d = a.shape
    tm = _pick_tile(lp, (528, 384, 352, 256))
    row_spec = pl.BlockSpec((1, tm, d), lambda b, i: (b, i, 0))
    vec_spec = pl.BlockSpec((1, d), lambda b, i: (0, 0))
    return pl.pallas_call(
        _out_proj_norms_kernel,
        grid=(bsz, lp // tm),
        in_specs=[row_spec, pl.BlockSpec((d, d), lambda b, i: (0, 0)),
                  _token_rows_spec(tm, d, lambda i: i), pl.BlockSpec((HEAD_ROWS, d), lambda b, i: (0, 0)),
                  vec_spec, vec_spec,
                  pl.BlockSpec((d, LANES), lambda b, i: (0, 0)), pl.BlockSpec((1, LANES), lambda b, i: (0, 0))],
        out_specs=[row_spec, row_spec, row_spec, pl.BlockSpec((1, tm, LANES), lambda b, i: (b, i, 0))],
        out_shape=[jax.ShapeDtypeStruct((bsz, lp, d), F32), jax.ShapeDtypeStruct((bsz, lp, d), BF16),
                   jax.ShapeDtypeStruct((bsz, lp, d), BF16), jax.ShapeDtypeStruct((bsz, lp, LANES), F32)],
        compiler_params=_params(("parallel", "parallel")),
        name="out_proj_norms",
    )(a, w, x, head, g_kv.reshape(1, d).astype(F32), g_b.reshape(1, d).astype(F32), w_f, b_f)


PROJ_COLS_PER_DOT = 512


def _head_proj_kernel(*refs, mode, scale):
    if mode == "headnorm":
        x_ref, w_ref, hg_ref, o_ref = refs
    else:
        x_ref, w_ref, o_ref = refs
    x = x_ref[0]
    n_out = w_ref.shape[1]
    cols_per_dot = min(PROJ_COLS_PER_DOT, n_out)
    assert n_out % cols_per_dot == 0
    heads_per_dot = cols_per_dot // HEAD_DIM
    for c in range(n_out // cols_per_dot):
        cols = pl.ds(c * cols_per_dot, cols_per_dot)
        acc = jnp.dot(x, w_ref[:, cols], preferred_element_type=F32)
        for n in range(heads_per_dot):
            a = acc[:, n * HEAD_DIM:(n + 1) * HEAD_DIM]
            head = c * heads_per_dot + n
            if mode == "headnorm":
                a = _rms(a) * hg_ref[:, head * HEAD_DIM:(head + 1) * HEAD_DIM]
                if scale is not None:
                    a = a * scale
            elif mode == "silu":
                a = a * _sigmoid(a)
            o_ref[0, head] = a.astype(o_ref.dtype)


def _head_proj(x, w, section, *, mode, out_dtype, head_gain=None, scale=None):
    bsz, lp, d = x.shape
    n_out = d
    tm = _pick_tile(lp, (1056, 704, 384, 128))
    in_specs = [pl.BlockSpec((1, tm, d), lambda b, i: (b, i, 0)),
                pl.BlockSpec((d, n_out), lambda b, i: (0, section))]
    args = [x, w]
    if mode == "headnorm":
        in_specs.append(pl.BlockSpec((1, n_out), lambda b, i: (0, 0)))
        args.append(head_gain.reshape(1, n_out).astype(F32))
    return pl.pallas_call(
        functools.partial(_head_proj_kernel, mode=mode, scale=scale),
        grid=(bsz, lp // tm),
        in_specs=in_specs,
        out_specs=pl.BlockSpec((1, n_out // HEAD_DIM, tm, HEAD_DIM), lambda b, i: (b, 0, i, 0)),
        out_shape=jax.ShapeDtypeStruct((bsz, n_out // HEAD_DIM, lp, HEAD_DIM), out_dtype),
        compiler_params=_params(("parallel", "parallel")),
        name=f"head_proj_{mode}",
    )(*args)


N_F_PIECES = 3


def _feature_scatter(n_heads):
    assert N_F_PIECES * n_heads < LANES
    r = jnp.arange(LANES)[:, None]
    c = jnp.arange(n_heads * LANES)[None, :]
    piece, src = r // n_heads, r % n_heads
    head, lane = c // LANES, c % LANES
    is_piece = (piece < N_F_PIECES) & (src == head)
    is_one = r == N_F_PIECES * n_heads
    sel_q = (is_piece & (lane == piece)) | (is_one & (lane >= N_F_PIECES) & (lane < 2 * N_F_PIECES))
    sel_k = (is_one & (lane < N_F_PIECES)).astype(F32) - (is_piece & (lane == N_F_PIECES + piece)).astype(F32)
    return sel_q.astype(BF16), sel_k.astype(BF16)


def _forget_features_kernel(lf_ref, selq_ref, selk_ref, qx_ref, kx_ref, carry_ref, *, n_heads, block_rows):
    i = pl.program_id(1)

    @pl.when(i == 0)
    def _():
        carry_ref[...] = jnp.zeros_like(carry_ref)

    t_idx = lax.broadcasted_iota(jnp.int32, (Q_BLOCK, Q_BLOCK), 0)
    s_idx = lax.broadcasted_iota(jnp.int32, (Q_BLOCK, Q_BLOCK), 1)
    tri = (s_idx <= t_idx).astype(F32)
    lane = lax.broadcasted_iota(jnp.int32, (Q_BLOCK, LANES), 1)
    pad_key = jnp.where(lane == N_F_PIECES, MASK_VALUE, 0.0)
    one_lane = jnp.where(lane == N_F_PIECES * n_heads, 1.0, 0.0)
    carry = carry_ref[...]
    for sub in range(block_rows // Q_BLOCK):
        rows = pl.ds(sub * Q_BLOCK, Q_BLOCK)
        row = i * block_rows + sub * Q_BLOCK + lax.broadcasted_iota(jnp.int32, (Q_BLOCK, 1), 0)
        real = row >= PAD_ROWS
        x = jnp.where(real, lf_ref[0, rows, :], 0.0)
        c = jnp.dot(tri, x, precision=lax.Precision.HIGHEST, preferred_element_type=F32) + carry
        carry = c[Q_BLOCK - 1:Q_BLOCK, :]
        f = jnp.where(lane < n_heads, c * LOG2E, 0.0)
        packed = one_lane
        for n in range(N_F_PIECES):
            piece = f.astype(BF16).astype(F32)
            f = f - piece
            packed = packed + (pltpu.roll(piece, n * n_heads, axis=1) if n else piece)
        packed = packed.astype(BF16)
        qx = jnp.dot(packed, selq_ref[...], preferred_element_type=F32)
        kx = jnp.dot(packed, selk_ref[...], preferred_element_type=F32)
        for h in range(n_heads):
            head = slice(h * LANES, (h + 1) * LANES)
            qx_ref[0, h, rows, :] = qx[:, head].astype(BF16)
            kx_ref[0, h, rows, :] = jnp.where(real, kx[:, head], pad_key).astype(BF16)
    carry_ref[...] = carry


def _forget_features(logf, n_heads):
    bsz, lp, _ = logf.shape
    block_rows = _pick_tile(lp, (384, 128))
    sel_q, sel_k = _feature_scatter(n_heads)
    out = jax.ShapeDtypeStruct((bsz, n_heads, lp, LANES), BF16)
    spec = pl.BlockSpec((1, n_heads, block_rows, LANES), lambda b, i: (b, 0, i, 0))
    sel_spec = pl.BlockSpec(sel_q.shape, lambda b, i: (0, 0))
    return pl.pallas_call(
        functools.partial(_forget_features_kernel, n_heads=n_heads, block_rows=block_rows),
        grid=(bsz, lp // block_rows),
        in_specs=[pl.BlockSpec((1, block_rows, LANES), lambda b, i: (b, i, 0)), sel_spec, sel_spec],
        out_specs=[spec, spec],
        out_shape=[out, out],
        scratch_shapes=[pltpu.VMEM((1, LANES), F32)],
        compiler_params=_params(("parallel", "arbitrary")),
        name="forget_features",
    )(logf, sel_q, sel_k)


SCORE_LOOKAHEAD = 2
KEY_TILE = 256


def _key_tiles(q_first, q_rows):
    tiles, start, end = [], 0, q_first + q_rows
    while start < end:
        width = min(KEY_TILE, end - start)
        masked = start + width - 1 > q_first
        tiles.append((start, width, q_first - start if masked else None))
        start += width
    return tiles


def _fox_attention_kernel(q_ref, qx_ref, k_ref, kx_ref, v_ref, gate_ref, o_ref, *, tile):
    nt_dims = (((1,), (1,)), ((), ()))
    n_tiles = q_ref.shape[2] // tile
    col_minus_row = (lax.broadcasted_iota(jnp.int32, (tile, KEY_TILE), 1)
                     - lax.broadcasted_iota(jnp.int32, (tile, KEY_TILE), 0))

    def update(carry, s, start, width):
        m_old, l, acc = carry
        blocks = [s[:, c * LANES:(c + 1) * LANES] for c in range(width // LANES)]
        row_max = jnp.max(functools.reduce(jnp.maximum, blocks), axis=-1, keepdims=True)
        m_new = jnp.maximum(m_old, row_max)
        alpha = jnp.exp2(m_old - m_new)
        p_blocks = [jnp.exp2(blk - m_new) for blk in blocks]
        l = alpha * l + functools.reduce(jnp.add, p_blocks)
        p = jnp.concatenate(p_blocks, axis=1).astype(BF16)
        acc = alpha * acc + jnp.dot(p, v_ref[0, 0, pl.ds(start, width), :], preferred_element_type=F32)
        return m_new, l, acc

    def scores(qi, start, width, offset):
        q_rows = pl.ds(qi * tile, tile)
        k_rows = pl.ds(start, width)
        q = jnp.concatenate([q_ref[0, 0, q_rows, :], qx_ref[0, 0, q_rows, :]], axis=1)
        kt = jnp.concatenate([k_ref[0, 0, k_rows, :], kx_ref[0, 0, k_rows, :]], axis=1)
        s = lax.dot_general(q, kt, nt_dims, preferred_element_type=F32)
        if offset is None:
            return s
        return jnp.where(col_minus_row[:, :width] <= offset, s, MASK_VALUE)

    pairs = []
    for qi in range(n_tiles):
        tiles = _key_tiles(qi * tile, tile)
        pairs += [(qi, n == 0, n == len(tiles) - 1, t) for n, t in enumerate(tiles)]
    pending = [scores(qi, *t) for qi, _, _, t in pairs[:SCORE_LOOKAHEAD]]
    carry = None
    for n, (qi, is_first, is_last, (start, width, _)) in enumerate(pairs):
        if n + SCORE_LOOKAHEAD < len(pairs):
            nxt = pairs[n + SCORE_LOOKAHEAD]
            pending.append(scores(nxt[0], *nxt[3]))
        if is_first:
            carry = (jnp.full((tile, LANES), -jnp.inf, F32), jnp.zeros((tile, LANES), F32),
                     jnp.zeros((tile, HEAD_DIM), F32))
        carry = update(carry, pending.pop(0), start, width)
        if is_last:
            _, l, acc = carry
            l = jnp.sum(l, axis=-1, keepdims=True)
            q_rows = pl.ds(qi * tile, tile)
            out = ((acc / l) * gate_ref[0, 0, q_rows, :]).astype(o_ref.dtype)
            first = qi * tile
            skip = max(HEAD_ROWS - first, 0)
            if skip < tile:
                o_ref[0, pl.ds(first + skip - HEAD_ROWS, tile - skip), :] = out[skip:]


def _fox_attention(q, qx, k, kx, v, gate):
    bsz, n_heads, lp, _ = q.shape
    tile = _pick_tile(lp, (384, 128))
    spec = pl.BlockSpec((1, 1, lp, HEAD_DIM), lambda b, h: (b, h, 0, 0))
    return pl.pallas_call(
        functools.partial(_fox_attention_kernel, tile=tile),
        grid=(bsz, n_heads),
        in_specs=[spec] * 6,
        out_specs=pl.BlockSpec((1, lp - HEAD_ROWS, HEAD_DIM), lambda b, h: (b, 0, h)),
        out_shape=jax.ShapeDtypeStruct((bsz, lp - HEAD_ROWS, n_heads * HEAD_DIM), BF16),
        compiler_params=_params(("parallel", "parallel")),
        name="fox_attention",
    )(q, qx, k, kx, v, gate)


def _out_proj_final_kernel(a_ref, w_ref, r_ref, o_ref):
    o_ref[0] = r_ref[0] + jnp.dot(a_ref[0], w_ref[...], preferred_element_type=F32)


def _out_proj_final(a, w, h):
    bsz, seq, d = a.shape
    tm = _pick_tile(seq, (512, 256, 128))
    return pl.pallas_call(
        _out_proj_final_kernel,
        grid=(bsz, seq // tm),
        in_specs=[pl.BlockSpec((1, tm, d), lambda b, i: (b, i, 0)),
                  pl.BlockSpec((d, d), lambda b, i: (0, 0)),
                  pl.BlockSpec((pl.Element(1), pl.Element(tm), pl.Element(d)),
                               lambda b, i: (b, pl.multiple_of(HEAD_ROWS + i * tm, HEAD_ROWS), 0))],
        out_specs=pl.BlockSpec((1, tm, d), lambda b, i: (b, i, 0)),
        out_shape=jax.ShapeDtypeStruct((bsz, seq, d), F32),
        compiler_params=_params(("parallel", "parallel")),
        name="out_proj_final",
    )(a, w, h)


def kernel(x, meta, gamma_lb, a_norm, a_w_in, a_out_norm, a_w_out, kv_norm, kv_w, fox_b_f, fox_k_norm,
           b_norm, b_w_in, b_q_norm, b_w_out):
    bsz, seq, d = x.shape
    n_heads = d // HEAD_DIM
    assert a_norm.shape[0] == 1 and b_norm.shape[0] == 1, "one HGRN2 layer followed by one attention layer"
    assert x.dtype == F32

    head = jnp.concatenate([jnp.zeros((PAD_ROWS, d), F32), meta.astype(F32)], axis=0)

    g = _hgrn2_layer(x, head, a_norm[0], a_w_in[0], gamma_lb, a_out_norm[0])

    w_f = jnp.zeros((d, LANES), BF16).at[:, :n_heads].set(kv_w[:, 2 * d:].astype(BF16))
    b_f = jnp.zeros((1, LANES), F32).at[0, :n_heads].set(fox_b_f.astype(F32))
    h1, x_kv, x_b, logf = _out_proj_norms(g, a_w_out[0].astype(BF16), x, head, kv_norm, b_norm[0], w_f, b_f)

    kv_w16 = kv_w.astype(BF16)
    k_s = _head_proj(x_kv, kv_w16, 0, mode="headnorm", out_dtype=BF16, head_gain=fox_k_norm)
    v_s = _head_proj(x_kv, kv_w16, 1, mode="raw", out_dtype=BF16)
    qx, kx = _forget_features(logf, n_heads)
    b_w16 = b_w_in[0].astype(BF16)
    q = _head_proj(x_b, b_w16, 0, mode="headnorm", out_dtype=BF16,
                   head_gain=b_q_norm[0], scale=HEAD_DIM ** -0.5 * LOG2E)
    gate = _head_proj(x_b, b_w16, 1, mode="silu", out_dtype=F32)
    o = _fox_attention(q, qx, k_s, kx, v_s, gate)
    return _out_proj_final(o, b_w_out[0].astype(BF16), h1)
```
